```python
import jax, jax.numpy as jnp
from jax import lax
import numpy as np

D_MODEL = 1024
BATCH = 2
SEQ = 8192
DEPTH = 2
DEC_BATCH = 32
DEC_SEQ = 4
PAST_LEN = 8192
PAGE_SIZE = 128

N_META = 16
GLA_HEADS = 4
GLA_DK = D_MODEL // 2 // GLA_HEADS
GLA_DV = D_MODEL // GLA_HEADS
GLA_RANK = 16
GLA_TAU = 16.0
GLA_CHUNK = 64
FOX_HEADS = 8
FOX_HD = 64
FOX_BLOCK = 128
D_FF = 11 * D_MODEL // 4
N_EXPERTS = 8
TOP_K = 2
D_FF_EXPERT = 7 * D_MODEL // 2
RMS_EPS = 1e-6

GLA_QK = GLA_HEADS * GLA_DK
GLA_V = GLA_HEADS * GLA_DV
FOX_W = FOX_HEADS * FOX_HD
IN_SPLITS = (GLA_QK, GLA_QK, GLA_V, GLA_V, GLA_RANK, FOX_W, FOX_W, FOX_W, FOX_HEADS, D_MODEL, D_MODEL)
N_IN = sum(IN_SPLITS)

kernel_name = "hybrid_gla_fox_gated_decoder_step"


def rmsnorm(x, g):
    xf = x.astype(jnp.float32)
    y = xf * lax.rsqrt(jnp.mean(xf * xf, axis=-1, keepdims=True) + RMS_EPS)
    return (y * g.astype(jnp.float32)).astype(x.dtype)


def _heads(a, h, d):
    return a.reshape(a.shape[:-1] + (h, d))


def in_projection(h, w_in, w_alpha2, b_alpha, b_f):
    z = h @ w_in
    offs = [int(o) for o in np.cumsum(IN_SPLITS)[:-1]]
    qa, ka, va, ra, a_lr, qb, kb, vb, fb, ga, gb = jnp.split(z, offs, axis=-1)
    log_alpha = jax.nn.log_sigmoid((a_lr @ w_alpha2 + b_alpha).astype(jnp.float32)) / GLA_TAU
    log_f = jax.nn.log_sigmoid((fb + b_f).astype(jnp.float32))
    gla = (_heads(qa, GLA_HEADS, GLA_DK) * GLA_DK ** -0.5,
           _heads(ka, GLA_HEADS, GLA_DK),
           _heads(va, GLA_HEADS, GLA_DV),
           _heads(log_alpha, GLA_HEADS, GLA_DK))
    fox = (_heads(qb, FOX_HEADS, FOX_HD), _heads(kb, FOX_HEADS, FOX_HD),
           _heads(vb, FOX_HEADS, FOX_HD), log_f)
    return gla, ra, fox, ga, gb


def gla_chunk(S0, q, k, v, log_a):
    q = q.astype(jnp.float32)
    k = k.astype(jnp.float32)
    v = v.astype(jnp.float32)
    L = q.shape[1]
    cum = jnp.cumsum(log_a, axis=1)
    causal = jnp.tril(jnp.ones((L, L), bool))[None, :, :, None, None]
    diff = cum[:, :, None] - cum[:, None, :]
    decay = jnp.exp(jnp.where(causal, diff, -jnp.inf))
    scores = jnp.einsum('bthd,bshd,btshd->bhts', q, k, decay)
    o = jnp.einsum('bhts,bshe->bthe', scores, v)
    o = o + jnp.einsum('bthd,bhde->bthe', q * jnp.exp(cum), S0)
    last = cum[:, -1]
    k_dec = k * jnp.exp(last[:, None] - cum)
    S1 = jnp.exp(last)[..., None] * S0 + jnp.einsum('bshd,bshe->bhde', k_dec, v)
    return S1, o


def gla_prompt(q, k, v, log_a):
    B, T = q.shape[:2]
    S0 = jnp.zeros((B, GLA_HEADS, GLA_DK, GLA_DV), jnp.float32)
    S, o_meta = gla_chunk(S0, q[:, :N_META], k[:, :N_META], v[:, :N_META], log_a[:, :N_META])
    n = (T - N_META) // GLA_CHUNK

    def chunks(a):
        a = a[:, N_META:]
        return a.reshape((B, n, GLA_CHUNK) + a.shape[2:]).swapaxes(0, 1)

    def step(S, xs):
        return gla_chunk(S, *xs)

    S, o = lax.scan(step, S, (chunks(q), chunks(k), chunks(v), chunks(log_a)))
    o = o.swapaxes(0, 1).reshape(B, T - N_META, GLA_HEADS, GLA_DV)
    return S, jnp.concatenate([o_meta, o], axis=1)


def fox_attend(q, cq, qpos, k, v, ck, kpos):
    s = jnp.einsum('bqhd,bkhd->bhqk', q, k).astype(jnp.float32) * FOX_HD ** -0.5
    bias = jnp.transpose(cq, (0, 2, 1))[..., :, None] - jnp.transpose(ck, (0, 2, 1))[..., None, :]
    mask = (kpos[None, :] <= qpos[:, None])[None, None]
    p = jax.nn.softmax(jnp.where(mask, s + bias, -jnp.inf), axis=-1)
    return jnp.einsum('bhqk,bkhd->bqhd', p.astype(v.dtype), v)


def fox_prompt(q, k, v, log_f):
    B, T = q.shape[:2]
    c = jnp.cumsum(log_f, axis=1)
    pos = jnp.arange(T)
    o_meta = fox_attend(q[:, :N_META], c[:, :N_META], pos[:N_META],
                        k[:, :N_META], v[:, :N_META], c[:, :N_META], pos[:N_META])
    n = (T - N_META) // FOX_BLOCK
    qb = q[:, N_META:].reshape(B, n, FOX_BLOCK, FOX_HEADS, FOX_HD).swapaxes(0, 1)
    cb = c[:, N_META:].reshape(B, n, FOX_BLOCK, FOX_HEADS).swapaxes(0, 1)
    pb = pos[N_META:].reshape(n, FOX_BLOCK)
    o = lax.map(lambda a: fox_attend(a[0], a[1], a[2], k, v, c, pos), (qb, cb, pb))
    o = o.swapaxes(0, 1).reshape(B, T - N_META, FOX_HEADS, FOX_HD)
    return jnp.concatenate([o_meta, o], axis=1)


def fox_sample(q, k, v, log_f, past_k, past_v, past_logf):
    P = past_k.shape[1]
    k_all = jnp.concatenate([past_k, k], axis=1)
    v_all = jnp.concatenate([past_v, v], axis=1)
    c = jnp.cumsum(jnp.concatenate([past_logf, log_f], axis=1), axis=1)
    pos = jnp.arange(k_all.shape[1])
    return fox_attend(q, c[:, P:], pos[P:], k_all, v_all, c, pos)


def merge_branches(o_gla, r, o_fox, ga, gb, gla_norm, w_proj_a, w_proj_b, w_out):
    dt = r.dtype
    o_a = rmsnorm(o_gla, gla_norm)
    o_a = o_a.reshape(o_a.shape[:-2] + (GLA_V,)).astype(dt) * jax.nn.silu(r)
    o_b = o_fox.reshape(o_fox.shape[:-2] + (FOX_W,)).astype(dt)
    y = jax.nn.sigmoid(ga) * (o_a @ w_proj_a) + jax.nn.sigmoid(gb) * (o_b @ w_proj_b)
    return y @ w_out


def swiglu(h, w1, w3, w2):
    return (jax.nn.silu(h @ w1) * (h @ w3)) @ w2


def moe_swiglu(h, w_router, w1, w3, w2):
    logits = (h @ w_router).astype(jnp.float32)
    top_v, top_i = lax.top_k(logits, TOP_K)
    top_g = jax.nn.softmax(top_v, axis=-1)
    gate = jnp.sum(jax.nn.one_hot(top_i, N_EXPERTS, dtype=jnp.float32) * top_g[..., None], axis=-2)
    out = jnp.zeros_like(h)
    for e in range(N_EXPERTS):
        out = out + gate[..., e:e + 1].astype(h.dtype) * swiglu(h, w1[e], w3[e], w2[e])
    return out


def setup_inputs(seed: int = 0) -> dict:
    key = jax.random.key(seed)
    ks = jax.random.split(key, 32)
    n_pages = PAST_LEN // PAGE_SIZE
    n_used = DEC_BATCH * n_pages
    n_pool = n_used + n_used // 4
    n_dense = (DEPTH + 1) // 2
    n_moe = DEPTH // 2
    f32 = jnp.float32

    def nrm(k, shape, scale):
        return jax.random.normal(k, shape, f32) * scale

    page_table = jax.random.permutation(ks[0], n_pool)[:n_used].reshape(DEC_BATCH, n_pages).astype(jnp.int32)
    return {
        "x_prompt": nrm(ks[1], (BATCH, SEQ, D_MODEL), 1.0),
        "x_sample": nrm(ks[2], (DEC_BATCH, DEC_SEQ, D_MODEL), 1.0),
        "cache_k": nrm(ks[3], (DEPTH, n_pool, PAGE_SIZE, FOX_HEADS, FOX_HD), 1.0),
        "cache_v": nrm(ks[4], (DEPTH, n_pool, PAGE_SIZE, FOX_HEADS, FOX_HD), 1.0),
        "cache_logf": jax.nn.log_sigmoid(3.0 + nrm(ks[5], (DEPTH, n_pool, PAGE_SIZE, FOX_HEADS), 1.0)),
        "state_gla": nrm(ks[6], (DEPTH, DEC_BATCH, GLA_HEADS, GLA_DK, GLA_DV), 1.0),
        "page_table": page_table,
        "meta_tokens": nrm(ks[7], (N_META, D_MODEL), 1.0),
        "norm_mix": 1.0 + nrm(ks[8], (DEPTH, D_MODEL), 0.02),
        "w_in": nrm(ks[9], (DEPTH, D_MODEL, N_IN), D_MODEL ** -0.5),
        "w_alpha2": nrm(ks[10], (DEPTH, GLA_RANK, GLA_QK), GLA_RANK ** -0.5),
        "b_alpha": nrm(ks[11], (DEPTH, GLA_QK), 0.1),
        "b_f": 3.0 + nrm(ks[12], (DEPTH, FOX_HEADS), 0.1),
        "gla_norm": 1.0 + nrm(ks[13], (DEPTH, GLA_DV), 0.02),
        "w_proj_a": nrm(ks[14], (DEPTH, GLA_V, D_MODEL), GLA_V ** -0.5),
        "w_proj_b": nrm(ks[15], (DEPTH, FOX_W, D_MODEL), FOX_W ** -0.5),
        "w_out": nrm(ks[16], (DEPTH, D_MODEL, D_MODEL), D_MODEL ** -0.5),
        "norm_ffn": 1.0 + nrm(ks[17], (DEPTH, D_MODEL), 0.02),
        "w_ffn1": nrm(ks[18], (n_dense, D_MODEL, D_FF), D_MODEL ** -0.5),
        "w_ffn3": nrm(ks[19], (n_dense, D_MODEL, D_FF), D_MODEL ** -0.5),
        "w_ffn2": nrm(ks[20], (n_dense, D_FF, D_MODEL), D_FF ** -0.5),
        "w_router": nrm(ks[21], (n_moe, D_MODEL, N_EXPERTS), D_MODEL ** -0.5),
        "w_exp1": nrm(ks[22], (n_moe, N_EXPERTS, D_MODEL, D_FF_EXPERT), D_MODEL ** -0.5),
        "w_exp3": nrm(ks[23], (n_moe, N_EXPERTS, D_MODEL, D_FF_EXPERT), D_MODEL ** -0.5),
        "w_exp2": nrm(ks[24], (n_moe, N_EXPERTS, D_FF_EXPERT, D_MODEL), D_FF_EXPERT ** -0.5),
        "norm_final": 1.0 + nrm(ks[25], (D_MODEL,), 0.02),
    }


def reference(x_prompt, x_sample, cache_k, cache_v, cache_logf, state_gla, page_table,
              meta_tokens, norm_mix, w_in, w_alpha2, b_alpha, b_f, gla_norm, w_proj_a, w_proj_b,
              w_out, norm_ffn, w_ffn1, w_ffn3, w_ffn2, w_router, w_exp1, w_exp3, w_exp2, norm_final):
    B = x_prompt.shape[0]
    DB = x_sample.shape[0]
    past = page_table.shape[1] * PAGE_SIZE
    xp = jnp.concatenate([jnp.broadcast_to(meta_tokens[None].astype(x_prompt.dtype), (B, N_META, D_MODEL)),
                          x_prompt], axis=1)
    xs = x_sample
    kp, vp, lp, sp, kss, vss, lss, sss = [], [], [], [], [], [], [], []
    for l in range(DEPTH):
        hp = rmsnorm(xp, norm_mix[l])
        hs = rmsnorm(xs, norm_mix[l])
        gla_p, r_p, fox_p, ga_p, gb_p = in_projection(hp, w_in[l], w_alpha2[l], b_alpha[l], b_f[l])
        gla_s, r_s, fox_s, ga_s, gb_s = in_projection(hs, w_in[l], w_alpha2[l], b_alpha[l], b_f[l])
        S_p, oa_p = gla_prompt(*gla_p)
        S_s, oa_s = gla_chunk(state_gla[l], *gla_s)
        ob_p = fox_prompt(*fox_p)
        past_k = cache_k[l, page_table].reshape(DB, past, FOX_HEADS, FOX_HD)
        past_v = cache_v[l, page_table].reshape(DB, past, FOX_HEADS, FOX_HD)
        past_lf = cache_logf[l, page_table].reshape(DB, past, FOX_HEADS)
        ob_s = fox_sample(*fox_s, past_k, past_v, past_lf)
        xp = xp + merge_branches(oa_p, r_p, ob_p, ga_p, gb_p, gla_norm[l], w_proj_a[l], w_proj_b[l], w_out[l])
        xs = xs + merge_branches(oa_s, r_s, ob_s, ga_s, gb_s, gla_norm[l], w_proj_a[l], w_proj_b[l], w_out[l])
        kp.append(fox_p[1]); vp.append(fox_p[2]); lp.append(fox_p[3]); sp.append(S_p)
        kss.append(fox_s[1]); vss.append(fox_s[2]); lss.append(fox_s[3]); sss.append(S_s)
        hp = rmsnorm(xp, norm_ffn[l])
        hs = rmsnorm(xs, norm_ffn[l])
        j = l // 2
        if l % 2 == 0:
            xp = xp + swiglu(hp, w_ffn1[j], w_ffn3[j], w_ffn2[j])
            xs = xs + swiglu(hs, w_ffn1[j], w_ffn3[j], w_ffn2[j])
        else:
            xp = xp + moe_swiglu(hp, w_router[j], w_exp1[j], w_exp3[j], w_exp2[j])
            xs = xs + moe_swiglu(hs, w_router[j], w_exp1[j], w_exp3[j], w_exp2[j])
    y_prompt = rmsnorm(xp, norm_final)[:, N_META:]
    y_sample = rmsnorm(xs, norm_final)
    return (y_prompt, y_sample,
            jnp.stack(kp), jnp.stack(vp), jnp.stack(lp), jnp.stack(sp),
            jnp.stack(kss), jnp.stack(vss), jnp.stack(lss), jnp.stack(sss))
```

```python
import functools

import jax
import jax.numpy as jnp
from jax import lax
from jax.experimental import pallas as pl
from jax.experimental.pallas import tpu as pltpu

BF = jnp.bfloat16
F32 = jnp.float32

D_MODEL = 1024
N_META = 16
BLK = 128
PAD = BLK - N_META
GLA_HEADS = 4
GLA_DK = 128
GLA_DV = 256
GLA_RANK = 16
GLA_TAU = 16.0
GLA_SUB = 16
FOX_HEADS = 8
FOX_HD = 64
N_EXPERTS = 8
RMS_EPS = 1e-6
NEG = -1e30
LANES = 128

GLA_QK = GLA_HEADS * GLA_DK
GLA_V = GLA_HEADS * GLA_DV
FOX_W = FOX_HEADS * FOX_HD
VA, RA, GA, GB = 0, 1024, 2048, 3072
QA, KA = 4096, 4608
QB, KB, VB = 5120, 5632, 6144
NZ = 6656
IN_SPLITS = (GLA_QK, GLA_QK, GLA_V, GLA_V, GLA_RANK, FOX_W, FOX_W, FOX_W, FOX_HEADS, D_MODEL, D_MODEL)

VMEM_LIMIT = 56 * 1024 * 1024


def _params(*sem):
    return pltpu.CompilerParams(dimension_semantics=sem, vmem_limit_bytes=VMEM_LIMIT)


def _tile(n, target, mult):
    best = None
    for t in range(mult, min(n, target) + 1, mult):
        if n % t == 0:
            best = t
    return best if best is not None else n


def _dot(a, b):
    return jnp.dot(a, b, preferred_element_type=F32)


def _dot_nt(a, b):
    return lax.dot_general(a, b, (((1,), (1,)), ((), ())), preferred_element_type=F32)


def _split3(x):
    h1 = x.astype(BF)
    r1 = x - h1.astype(F32)
    h2 = r1.astype(BF)
    r2 = r1 - h2.astype(F32)
    return h1, h2, r2.astype(BF)


def _dot_exact_r(x, m):
    h1, h2, h3 = _split3(x)
    return (_dot(h3, m) + _dot(h2, m)) + _dot(h1, m)


def _dot_exact_l(m, x):
    h1, h2, h3 = _split3(x)
    return (_dot(m, h3) + _dot(m, h2)) + _dot(m, h1)


def _logsig(x):
    return jnp.minimum(x, 0.0) - jnp.log1p(jnp.exp(-jnp.abs(x)))


def _rms(x, g):
    ms = jnp.mean(x * x, axis=-1, keepdims=True)
    return x * lax.rsqrt(ms + RMS_EPS) * g


def _inproj_kernel(x_ref, g_ref, wm_ref, ws_ref, wa_ref, ba_ref, bf_ref, z_ref, la_ref, lf_ref, h_scr):
    @pl.when(pl.program_id(1) == 0)
    def _():
        h = _rms(x_ref[...], g_ref[...]).astype(BF)
        h_scr[...] = h
        s = _dot(h, ws_ref[...])
        lf_ref[...] = _logsig(s + bf_ref[...])
        a = _dot(s.astype(BF), wa_ref[...]) + ba_ref[...]
        la_ref[...] = _logsig(a) * (1.0 / GLA_TAU)

    z_ref[...] = _dot(h_scr[...], wm_ref[...])


def _inproj(x2, g, wm, ws, wa, ba, bfp):
    m = x2.shape[0]
    tm = _tile(m, 1280, 8)
    tn = _tile(NZ, 1664, LANES)
    return pl.pallas_call(
        _inproj_kernel,
        grid=(m // tm, NZ // tn),
        in_specs=[
            pl.BlockSpec((tm, D_MODEL), lambda i, j: (i, 0)),
            pl.BlockSpec((1, D_MODEL), lambda i, j: (0, 0)),
            pl.BlockSpec((D_MODEL, tn), lambda i, j: (0, j)),
            pl.BlockSpec((D_MODEL, LANES), lambda i, j: (0, 0)),
            pl.BlockSpec((LANES, GLA_QK), lambda i, j: (0, 0)),
            pl.BlockSpec((1, GLA_QK), lambda i, j: (0, 0)),
            pl.BlockSpec((1, LANES), lambda i, j: (0, 0)),
        ],
        out_specs=[
            pl.BlockSpec((tm, tn), lambda i, j: (i, j)),
            pl.BlockSpec((tm, GLA_QK), lambda i, j: (i, 0)),
            pl.BlockSpec((tm, LANES), lambda i, j: (i, 0)),
        ],
        out_shape=[
            jax.ShapeDtypeStruct((m, NZ), F32),
            jax.ShapeDtypeStruct((m, GLA_QK), F32),
            jax.ShapeDtypeStruct((m, LANES), F32),
        ],
        scratch_shapes=[pltpu.VMEM((tm, D_MODEL), BF)],
        compiler_params=_params("parallel", "arbitrary"),
        name="inproj",
    )(x2, g, wm, ws, wa, ba, bfp)


def _cumsum_kernel(lf_ref, c_ref, carry, *, nsub):
    j = pl.program_id(1)

    @pl.when(j == 0)
    def _():
        carry[...] = jnp.zeros_like(carry)

    r = lax.broadcasted_iota(jnp.int32, (BLK, BLK), 0)
    c = lax.broadcasted_iota(jnp.int32, (BLK, BLK), 1)
    upper = (r <= c).astype(BF)
    for u in range(nsub):
        x = lf_ref[0, u * BLK:(u + 1) * BLK, :]
        pos = (j * nsub + u) * BLK + r
        x = jnp.where(pos >= PAD, x, 0.0)
        cb = _dot_exact_r(x.T, upper) + carry[...]
        c_ref[0, :, u * BLK:(u + 1) * BLK] = cb
        carry[...] = cb[:, BLK - 1:BLK]


def _fox_cumsum(lf3):
    b, t, _ = lf3.shape
    tb = _tile(t, 640, BLK)
    return pl.pallas_call(
        functools.partial(_cumsum_kernel, nsub=tb // BLK),
        grid=(b, t // tb),
        in_specs=[pl.BlockSpec((1, tb, LANES), lambda i, j: (i, j, 0))],
        out_specs=pl.BlockSpec((1, LANES, tb), lambda i, j: (i, 0, j)),
        out_shape=jax.ShapeDtypeStruct((b, LANES, t), F32),
        scratch_shapes=[pltpu.VMEM((LANES, 1), F32)],
        compiler_params=_params("parallel", "arbitrary"),
        name="fox_cumsum",
    )(lf3)


def _fox_prompt_kernel(q_ref, k_ref, v_ref, ck_ref, o_ref, kbf, vbf, m_scr, l_scr, acc_scr, *, tq, tk):
    qi = pl.program_id(2)

    @pl.when(qi == 0)
    def _():
        kbf[...] = k_ref[0].astype(BF)
        vbf[...] = v_ref[0].astype(BF)

    lane = lax.broadcasted_iota(jnp.int32, (1, LANES), 1)
    q2 = q_ref[0] * (FOX_HD ** -0.5)
    qh = (jnp.where(lane < FOX_HD, q2, 0.0).astype(BF), jnp.where(lane >= FOX_HD, q2, 0.0).astype(BF))
    m_scr[...] = jnp.full(m_scr.shape, NEG, F32)
    l_scr[...] = jnp.zeros(l_scr.shape, F32)
    acc_scr[...] = jnp.zeros(acc_scr.shape, F32)
    row = qi * tq + lax.broadcasted_iota(jnp.int32, (tq, tk), 0)
    coli = lax.broadcasted_iota(jnp.int32, (tq, tk), 1)
    nkv = ((qi + 1) * tq + tk - 1) // tk

    def body(j, carry):
        start = pl.multiple_of(j * tk, tk)
        ks = kbf[pl.ds(start, tk), :]
        vs = vbf[pl.ds(start, tk), :]
        col = start + coli
        valid = (col <= row) & (col >= PAD)
        for h in range(2):
            s = _dot_nt(qh[h], ks) - ck_ref[0, 0, h:h + 1, pl.ds(start, tk)]
            s = jnp.where(valid, s, NEG)
            m_old = m_scr[h]
            m_new = jnp.maximum(m_old, jnp.max(s, axis=-1, keepdims=True))
            p = jnp.exp(s - m_new)
            alpha = jnp.exp(m_old - m_new)
            l_scr[h] = alpha * l_scr[h] + jnp.sum(p, axis=-1, keepdims=True)
            acc_scr[h] = alpha * acc_scr[h] + _dot(p.astype(BF), vs)
            m_scr[h] = m_new
        return carry

    lax.fori_loop(0, nkv, body, 0)
    o_ref[0] = jnp.where(lane < FOX_HD, acc_scr[0] / l_scr[0], acc_scr[1] / l_scr[1])


def _fox_prompt(z3, ck4):
    b, t, _ = z3.shape
    tq = _tile(t, 640, BLK)
    tk = tq
    npair = FOX_HEADS // 2
    return pl.pallas_call(
        functools.partial(_fox_prompt_kernel, tq=tq, tk=tk),
        grid=(b, npair, t // tq),
        in_specs=[
            pl.BlockSpec((1, tq, LANES), lambda i, hp, qi: (i, qi, QB // LANES + hp)),
            pl.BlockSpec((1, t, LANES), lambda i, hp, qi: (i, 0, KB // LANES + hp)),
            pl.BlockSpec((1, t, LANES), lambda i, hp, qi: (i, 0, VB // LANES + hp)),
            pl.BlockSpec((1, 1, 2, t), lambda i, hp, qi: (i, hp, 0, 0)),
        ],
        out_specs=pl.BlockSpec((1, tq, LANES), lambda i, hp, qi: (i, qi, hp)),
        out_shape=jax.ShapeDtypeStruct((b, t, FOX_W), F32),
        scratch_shapes=[
            pltpu.VMEM((t, LANES), BF), pltpu.VMEM((t, LANES), BF),
            pltpu.VMEM((2, tq, 1), F32), pltpu.VMEM((2, tq, 1), F32), pltpu.VMEM((2, tq, LANES), F32),
        ],
        compiler_params=_params("parallel", "parallel", "arbitrary"),
        name="fox_prompt",
    )(z3, z3, z3, ck4)


def _gla_chunk_math(q, k, v, la, s0):
    ri = lax.broadcasted_iota(jnp.int32, (BLK, BLK), 0)
    ci = lax.broadcasted_iota(jnp.int32, (BLK, BLK), 1)
    lower = (ri >= ci).astype(BF)
    cum = _dot_exact_l(lower, la)
    cum_t, q_t, k_t = cum.T, q.T, k.T
    a = jnp.zeros((BLK, BLK), F32)
    for d in range(GLA_SUB):
        if d:
            cs = pltpu.roll(cum_t, BLK - d, 1)
            qs = pltpu.roll(q_t, BLK - d, 1)
        else:
            cs, qs = cum_t, q_t
        e = jnp.exp(jnp.minimum(cs - cum_t, 0.0))
        band = jnp.sum(qs * k_t * e, axis=0, keepdims=True)
        sel = ((ri - ci) == d) & (((ci & (GLA_SUB - 1)) + d) < GLA_SUB)
        a = jnp.where(sel, band, a)
    m = BLK // 2
    while m >= GLA_SUB:
        nb = BLK // (2 * m)
        pieces = [jnp.broadcast_to(cum[2 * m * u + m - 1:2 * m * u + m, :], (2 * m, GLA_DK)) for u in range(nb)]
        bnd = pieces[0] if nb == 1 else jnp.concatenate(pieces, axis=0)
        up = (ri & (2 * m - 1)) >= m
        dq = cum - bnd
        qt = jnp.where(up, q * jnp.exp(jnp.minimum(dq, 0.0)), 0.0).astype(BF)
        kt = jnp.where(up, 0.0, k * jnp.exp(jnp.minimum(-dq, 0.0))).astype(BF)
        al = _dot_nt(qt, kt)
        shift = (2 * m).bit_length() - 1
        pm = ((ri >> shift) == (ci >> shift)) & up & ((ci & (2 * m - 1)) < m)
        a = jnp.where(pm, al, a)
        m //= 2
    vb = v.astype(BF)
    o = _dot(a.astype(BF), vb) + _dot((q * jnp.exp(cum)).astype(BF), s0.astype(BF))
    last = cum_t[:, BLK - 1:BLK]
    kd = (k_t * jnp.exp(last - cum_t)).astype(BF)
    s1 = jnp.exp(last) * s0 + _dot(kd, vb)
    return o, s1


def _gla_prompt_kernel(q_ref, k_ref, v_ref, la_ref, o_ref, s_ref, s_scr):
    c = pl.program_id(2)

    @pl.when(c == 0)
    def _():
        s_scr[...] = jnp.zeros_like(s_scr)

    pos = c * BLK + lax.broadcasted_iota(jnp.int32, (BLK, GLA_DK), 0)
    q = q_ref[0] * (GLA_DK ** -0.5)
    k = jnp.where(pos >= PAD, k_ref[0], 0.0)
    o, s1 = _gla_chunk_math(q, k, v_ref[0], la_ref[0], s_scr[...])
    o_ref[0] = o
    s_scr[...] = s1
    s_ref[0, 0] = s1


def _gla_prompt(z3, la3):
    b, t, _ = z3.shape
    return pl.pallas_call(
        _gla_prompt_kernel,
        grid=(b, GLA_HEADS, t // BLK),
        in_specs=[
            pl.BlockSpec((1, BLK, GLA_DK), lambda i, h, c: (i, c, QA // GLA_DK + h)),
            pl.BlockSpec((1, BLK, GLA_DK), lambda i, h, c: (i, c, KA // GLA_DK + h)),
            pl.BlockSpec((1, BLK, GLA_DV), lambda i, h, c: (i, c, VA // GLA_DV + h)),
            pl.BlockSpec((1, BLK, GLA_DK), lambda i, h, c: (i, c, h)),
        ],
        out_specs=[
            pl.BlockSpec((1, BLK, GLA_DV), lambda i, h, c: (i, c, h)),
            pl.BlockSpec((1, 1, GLA_DK, GLA_DV), lambda i, h, c: (i, h, 0, 0)),
        ],
        out_shape=[
            jax.ShapeDtypeStruct((b, t, GLA_V), F32),
            jax.ShapeDtypeStruct((b, GLA_HEADS, GLA_DK, GLA_DV), F32),
        ],
        scratch_shapes=[pltpu.VMEM((GLA_DK, GLA_DV), F32)],
        compiler_params=_params("parallel", "parallel", "arbitrary"),
        name="gla_prompt",
    )(z3, z3, z3, la3)


def _gla_sample_kernel(q_ref, k_ref, v_ref, la_ref, s0_ref, o_ref, s1_ref, pq, pk, pc, pv, *, n):
    ri = lax.broadcasted_iota(jnp.int32, (n, GLA_DK), 0)
    for h in range(GLA_HEADS):
        q = q_ref[0, :, h * GLA_DK:(h + 1) * GLA_DK] * (GLA_DK ** -0.5)
        k = k_ref[0, :, h * GLA_DK:(h + 1) * GLA_DK]
        la = la_ref[0, :, h * GLA_DK:(h + 1) * GLA_DK]
        v = v_ref[0, :, h * GLA_DV:(h + 1) * GLA_DV]
        s0 = s0_ref[0, h]
        cum = jnp.zeros_like(la)
        for j in range(n):
            cum = cum + jnp.where(ri >= j, la[j:j + 1, :], 0.0)
        o = jnp.zeros((n, GLA_DV), F32)
        for s in range(n):
            w = q * jnp.exp(jnp.minimum(cum - cum[s:s + 1, :], 0.0)) * k[s:s + 1, :]
            sc = jnp.sum(w, axis=-1, keepdims=True)
            o = o + jnp.where(ri[:, :1] >= s, sc, 0.0) * v[s:s + 1, :]
        for scr, val in ((pq, q * jnp.exp(cum)), (pk, k), (pc, cum)):
            scr[...] = jnp.zeros_like(scr)
            scr[0:n, :] = val
        pv[...] = jnp.zeros_like(pv)
        pv[0:n, :] = v
        s0b = s0.astype(BF)
        o_inter = _dot(pq[...].astype(BF), s0b)
        o_ref[0, :, h * GLA_DV:(h + 1) * GLA_DV] = o + o_inter[0:n, :]
        cum_t = pc[...].T
        k_t = pk[...].T
        last = cum_t[:, n - 1:n]
        kd = (k_t * jnp.exp(jnp.minimum(last - cum_t, 0.0))).astype(BF)
        s1_ref[0, h] = jnp.exp(last) * s0 + _dot(kd, pv[...].astype(BF))


def _gla_sample(z3, la3, state):
    db, n, _ = z3.shape
    return pl.pallas_call(
        functools.partial(_gla_sample_kernel, n=n),
        grid=(db,),
        in_specs=[
            pl.BlockSpec((1, n, GLA_QK), lambda i: (i, 0, QA // GLA_QK)),
            pl.BlockSpec((1, n, GLA_QK), lambda i: (i, 0, KA // GLA_QK)),
            pl.BlockSpec((1, n, GLA_V), lambda i: (i, 0, VA // GLA_V)),
            pl.BlockSpec((1, n, GLA_QK), lambda i: (i, 0, 0)),
            pl.BlockSpec((1, GLA_HEADS, GLA_DK, GLA_DV), lambda i: (i, 0, 0, 0)),
        ],
        out_specs=[
            pl.BlockSpec((1, n, GLA_V), lambda i: (i, 0, 0)),
            pl.BlockSpec((1, GLA_HEADS, GLA_DK, GLA_DV), lambda i: (i, 0, 0, 0)),
        ],
        out_shape=[
            jax.ShapeDtypeStruct((db, n, GLA_V), F32),
            jax.ShapeDtypeStruct(state.shape, F32),
        ],
        scratch_shapes=[pltpu.VMEM((BLK, GLA_DK), F32), pltpu.VMEM((BLK, GLA_DK), F32),
                        pltpu.VMEM((BLK, GLA_DK), F32), pltpu.VMEM((BLK, GLA_DV), F32)],
        compiler_params=_params("parallel"),
        name="gla_sample",
    )(z3, z3, z3, la3, state)


def _pagecum_kernel(lf_ref, m_ref, c_ref):
    c_ref[0] = _dot_exact_r(lf_ref[0], m_ref[...])


def _page_cumsum(cache_logf):
    depth, n_pool = cache_logf.shape[:2]
    w = BLK * FOX_HEADS
    r = jnp.arange(w)
    mat = ((r[:, None] % FOX_HEADS == r[None, :] // BLK) & (r[:, None] // FOX_HEADS <= r[None, :] % BLK)).astype(BF)
    tp = _tile(n_pool, 512, 8)
    out = pl.pallas_call(
        _pagecum_kernel,
        grid=(depth, n_pool // tp),
        in_specs=[pl.BlockSpec((1, tp, w), lambda l, i: (l, i, 0)),
                  pl.BlockSpec((w, w), lambda l, i: (0, 0))],
        out_specs=pl.BlockSpec((1, tp, w), lambda l, i: (l, i, 0)),
        out_shape=jax.ShapeDtypeStruct((depth, n_pool, w), F32),
        compiler_params=_params("parallel", "parallel"),
        name="page_cumsum",
    )(cache_logf.reshape(depth, n_pool, w), mat)
    return out.reshape(depth, n_pool, FOX_HEADS, BLK)


def _fox_sample_kernel(pt_ref, *refs, pps, n, layer):
    del pt_ref, layer
    k_refs = refs[0:pps]
    v_refs = refs[pps:2 * pps]
    c_refs = refs[2 * pps:3 * pps]
    q_ref, kn_ref, vn_ref, lfn_ref, o_ref, qbd, carry, m_scr, l_scr, acc_scr = refs[3 * pps:]
    j = pl.program_id(1)
    rows = n * FOX_HEADS
    rr = lax.broadcasted_iota(jnp.int32, (rows, FOX_W), 0)
    cc = lax.broadcasted_iota(jnp.int32, (rows, FOX_W), 1)
    own = (cc >> 6) == (rr & (FOX_HEADS - 1))

    @pl.when(j == 0)
    def _():
        qrep = jnp.concatenate([jnp.broadcast_to(q_ref[0, t:t + 1, :], (FOX_HEADS, FOX_W)) for t in range(n)], axis=0)
        qbd[...] = jnp.where(own, qrep * (FOX_HD ** -0.5), 0.0)
        carry[...] = jnp.zeros_like(carry)
        m_scr[...] = jnp.full(m_scr.shape, NEG, F32)
        l_scr[...] = jnp.zeros_like(l_scr)
        acc_scr[...] = jnp.zeros_like(acc_scr)

    qb = qbd[...].astype(BF)
    for r in range(pps):
        kp = k_refs[r][0, 0].astype(BF)
        vp = v_refs[r][0, 0].astype(BF)
        cg = carry[...] + c_refs[r][0, 0]
        carry[...] = cg[:, BLK - 1:BLK]
        s = _dot_nt(qb, kp) - jnp.concatenate([cg] * n, axis=0)
        m_old = m_scr[...]
        m_new = jnp.maximum(m_old, jnp.max(s, axis=-1, keepdims=True))
        p = jnp.exp(s - m_new)
        alpha = jnp.exp(m_old - m_new)
        l_scr[...] = alpha * l_scr[...] + jnp.sum(p, axis=-1, keepdims=True)
        acc_scr[...] = alpha * acc_scr[...] + _dot(p.astype(BF), vp)
        m_scr[...] = m_new

    @pl.when(j == pl.num_programs(1) - 1)
    def _():
        lfn = lfn_ref[0]
        ri = lax.broadcasted_iota(jnp.int32, (n, LANES), 0)
        cn = jnp.zeros_like(lfn)
        for i in range(n):
            cn = cn + jnp.where(ri >= i, lfn[i:i + 1, :], 0.0)
        er = lax.broadcasted_iota(jnp.int32, (rows, LANES), 0)
        ec = lax.broadcasted_iota(jnp.int32, (rows, LANES), 1)
        sel = (er & (FOX_HEADS - 1)) == ec
        tot = jnp.concatenate([carry[...]] * n, axis=0)
        qf = qbd[...]
        tok = rr[:, :1] >> 3
        m_run, l_run, acc = m_scr[...], l_scr[...], acc_scr[...]
        for i in range(n):
            ci = jnp.sum(jnp.where(sel, cn[i:i + 1, :], 0.0), axis=-1, keepdims=True)
            s = jnp.sum(qf * kn_ref[0, i:i + 1, :], axis=-1, keepdims=True) - (tot + ci)
            s = jnp.where(tok >= i, s, NEG)
            m_new = jnp.maximum(m_run, s)
            p = jnp.exp(s - m_new)
            alpha = jnp.exp(m_run - m_new)
            l_run = alpha * l_run + p
            acc = alpha * acc + p * vn_ref[0, i:i + 1, :]
            m_run = m_new
        o_full = jnp.where(own, acc / l_run, 0.0)
        o_ref[0] = jnp.sum(o_full.reshape(n, FOX_HEADS, FOX_W), axis=1)


def _fox_sample(page_flat, cache_k4, cache_v4, cpage, layer, q3, kn3, vn3, lfn3):
    db, n, _ = q3.shape
    n_pages = page_flat.shape[0] // db
    pps = _tile(n_pages, 8, 1)
    rows = n * FOX_HEADS

    def page_map(r):
        return lambda i, j, pt: (layer, pt[i * n_pages + j * pps + r], 0, 0)

    def kv_specs():
        return [pl.BlockSpec((1, 1, BLK, FOX_W), page_map(r)) for r in range(pps)]

    c_specs = [pl.BlockSpec((1, 1, FOX_HEADS, BLK), page_map(r)) for r in range(pps)]
    grid_spec = pltpu.PrefetchScalarGridSpec(
        num_scalar_prefetch=1,
        grid=(db, n_pages // pps),
        in_specs=kv_specs() + kv_specs() + c_specs + [
            pl.BlockSpec((1, n, FOX_W), lambda i, j, pt: (i, 0, QB // FOX_W)),
            pl.BlockSpec((1, n, FOX_W), lambda i, j, pt: (i, 0, 0)),
            pl.BlockSpec((1, n, FOX_W), lambda i, j, pt: (i, 0, 0)),
            pl.BlockSpec((1, n, LANES), lambda i, j, pt: (i, 0, 0)),
        ],
        out_specs=pl.BlockSpec((1, n, FOX_W), lambda i, j, pt: (i, 0, 0)),
        scratch_shapes=[
            pltpu.VMEM((rows, FOX_W), F32), pltpu.VMEM((FOX_HEADS, 1), F32),
            pltpu.VMEM((rows, 1), F32), pltpu.VMEM((rows, 1), F32), pltpu.VMEM((rows, FOX_W), F32),
        ],
    )
    return pl.pallas_call(
        functools.partial(_fox_sample_kernel, pps=pps, n=n, layer=layer),
        grid_spec=grid_spec,
        out_shape=jax.ShapeDtypeStruct((db, n, FOX_W), F32),
        compiler_params=_params("parallel", "arbitrary"),
        name="fox_sample",
    )(page_flat, *([cache_k4] * pps), *([cache_v4] * pps), *([cpage] * pps), q3, kn3, vn3, lfn3)


def _merge_kernel(*refs, moe):
    (x_ref, og_ref, r_ref, ob_ref, ga_ref, gb_ref, gn_ref, wpa_ref, wpb_ref, wo_ref, nf_ref) = refs[:11]
    if moe:
        wr_ref, xo_ref, h_ref, idx_ref, gate_ref = refs[11:]
    else:
        xo_ref, h_ref = refs[11:]
    og = og_ref[...]
    gn = gn_ref[...]
    oa = jnp.concatenate([_rms(og[:, h * GLA_DV:(h + 1) * GLA_DV], gn) for h in range(GLA_HEADS)], axis=-1)
    oa = oa * jax.nn.silu(r_ref[...])
    pa = _dot(oa.astype(BF), wpa_ref[...])
    pb = _dot(ob_ref[...].astype(BF), wpb_ref[...])
    y = jax.nn.sigmoid(ga_ref[...]) * pa + jax.nn.sigmoid(gb_ref[...]) * pb
    xn = x_ref[...] + _dot(y.astype(BF), wo_ref[...])
    xo_ref[...] = xn
    h = _rms(xn, nf_ref[...])
    h_ref[...] = h.astype(BF)
    if moe:
        h1 = h.astype(BF)
        h2 = (h - h1.astype(F32)).astype(BF)
        w1 = wr_ref[0]
        w2 = wr_ref[1]
        logits = (_dot(h2, w1) + _dot(h1, w2)) + _dot(h1, w1)
        lane = lax.broadcasted_iota(jnp.int32, logits.shape, 1)
        logits = jnp.where(lane < N_EXPERTS, logits, -jnp.inf)
        v1 = jnp.max(logits, axis=-1, keepdims=True)
        i1 = jnp.min(jnp.where(logits == v1, lane, LANES), axis=-1, keepdims=True)
        rest = jnp.where(lane == i1, -jnp.inf, logits)
        v2 = jnp.max(rest, axis=-1, keepdims=True)
        i2 = jnp.min(jnp.where(rest == v2, lane, LANES), axis=-1, keepdims=True)
        e2 = jnp.exp(v2 - v1)
        g1 = 1.0 / (1.0 + e2)
        g2 = e2 / (1.0 + e2)
        idx_ref[...] = jnp.where(lane == 0, i1, jnp.where(lane == 1, i2, 0))
        gate_ref[...] = jnp.where(lane == 0, g1, jnp.where(lane == 1, g2, 0.0))


def _merge(x2, z, og, ob, gn, wpa, wpb, wo, nf, wr=None):
    m = x2.shape[0]
    moe = wr is not None
    tm = _tile(m, 320, 8)
    cb = D_MODEL
    row = lambda c: pl.BlockSpec((tm, cb), lambda i: (i, c))
    full = lambda a: pl.BlockSpec(a.shape, lambda i: (0,) * a.ndim)
    in_specs = [row(0), row(0), row(RA // cb), pl.BlockSpec((tm, FOX_W), lambda i: (i, 0)),
                pl.BlockSpec((tm, cb), lambda i: (i, GA // cb)),
                pl.BlockSpec((tm, cb), lambda i: (i, GB // cb)),
                full(gn), full(wpa), full(wpb), full(wo), full(nf)]
    args = [x2, og, z, ob, z, z, gn, wpa, wpb, wo, nf]
    out_specs = [row(0), row(0)]
    out_shape = [jax.ShapeDtypeStruct((m, D_MODEL), F32), jax.ShapeDtypeStruct((m, D_MODEL), BF)]
    if moe:
        in_specs.append(full(wr))
        args.append(wr)
        out_specs += [pl.BlockSpec((tm, LANES), lambda i: (i, 0))] * 2
        out_shape += [jax.ShapeDtypeStruct((m, LANES), jnp.int32), jax.ShapeDtypeStruct((m, LANES), F32)]
    return pl.pallas_call(
        functools.partial(_merge_kernel, moe=moe),
        grid=(m // tm,),
        in_specs=in_specs, out_specs=out_specs, out_shape=out_shape,
        compiler_params=_params("parallel"),
        name="merge",
    )(*args)


def _ffn_kernel(h_ref, x_ref, w1_ref, w3_ref, w2_ref, o_ref, acc):
    j = pl.program_id(1)

    @pl.when(j == 0)
    def _():
        acc[...] = jnp.zeros_like(acc)

    h = h_ref[...]
    u = jax.nn.silu(_dot(h, w1_ref[...])) * _dot(h, w3_ref[...])
    acc[...] += _dot(u.astype(BF), w2_ref[...])

    @pl.when(j == pl.num_programs(1) - 1)
    def _():
        o_ref[...] = x_ref[...] + acc[...]


def _ffn(h, x2, w1, w3, w2):
    m = x2.shape[0]
    f = w1.shape[1]
    tm = _tile(m, 1280, 8)
    tf = _tile(f, 256, LANES)
    return pl.pallas_call(
        _ffn_kernel,
        grid=(m // tm, f // tf),
        in_specs=[
            pl.BlockSpec((tm, D_MODEL), lambda i, j: (i, 0)),
            pl.BlockSpec((tm, D_MODEL), lambda i, j: (i, 0)),
            pl.BlockSpec((D_MODEL, tf), lambda i, j: (0, j)),
            pl.BlockSpec((D_MODEL, tf), lambda i, j: (0, j)),
            pl.BlockSpec((tf, D_MODEL), lambda i, j: (j, 0)),
        ],
        out_specs=pl.BlockSpec((tm, D_MODEL), lambda i, j: (i, 0)),
        out_shape=jax.ShapeDtypeStruct((m, D_MODEL), F32),
        scratch_shapes=[pltpu.VMEM((tm, D_MODEL), F32)],
        compiler_params=_params("parallel", "arbitrary"),
        name="ffn",
    )(h, x2, w1, w3, w2)


def _moe_kernel(te_ref, tv_ref, h_ref, g_ref, w1_ref, w3_ref, w2_ref, o_ref, acc):
    del te_ref
    i = pl.program_id(0)
    j = pl.program_id(1)
    live = tv_ref[i] > 0

    @pl.when(j == 0)
    def _():
        acc[...] = jnp.zeros_like(acc)

    @pl.when(live)
    def _():
        h = h_ref[...]
        u = jax.nn.silu(_dot(h, w1_ref[0])) * _dot(h, w3_ref[0])
        acc[...] += _dot(u.astype(BF), w2_ref[0])

    @pl.when(j == pl.num_programs(1) - 1)
    def _():
        o_ref[...] = g_ref[...] * acc[...]


def _moe_experts(tile_expert, tile_valid, hs, gs, w1, w3, w2, tm):
    rows = hs.shape[0]
    f = w1.shape[2]
    tf = _tile(f, 512, LANES)
    nf = f // tf

    def fj(i, j, tv):
        return jnp.where(tv[i] > 0, j, nf - 1)

    grid_spec = pltpu.PrefetchScalarGridSpec(
        num_scalar_prefetch=2,
        grid=(rows // tm, nf),
        in_specs=[
            pl.BlockSpec((tm, D_MODEL), lambda i, j, te, tv: (i, 0)),
            pl.BlockSpec((tm, 1), lambda i, j, te, tv: (i, 0)),
            pl.BlockSpec((1, D_MODEL, tf), lambda i, j, te, tv: (te[i], 0, fj(i, j, tv))),
            pl.BlockSpec((1, D_MODEL, tf), lambda i, j, te, tv: (te[i], 0, fj(i, j, tv))),
            pl.BlockSpec((1, tf, D_MODEL), lambda i, j, te, tv: (te[i], fj(i, j, tv), 0)),
        ],
        out_specs=pl.BlockSpec((tm, D_MODEL), lambda i, j, te, tv: (i, 0)),
        scratch_shapes=[pltpu.VMEM((tm, D_MODEL), F32)],
    )
    return pl.pallas_call(
        _moe_kernel,
        grid_spec=grid_spec,
        out_shape=jax.ShapeDtypeStruct((rows, D_MODEL), F32),
        compiler_params=_params("parallel", "arbitrary"),
        name="moe_experts",
    )(tile_expert, tile_valid, hs, gs, w1, w3, w2)


def _moe_route(idx, gate, tm):
    mt = idx.shape[0]
    e_flat = jnp.concatenate([idx[:, 0], idx[:, 1]])
    g_flat = jnp.concatenate([gate[:, 0], gate[:, 1]])
    tok = jnp.concatenate([jnp.arange(mt, dtype=jnp.int32)] * 2)
    onehot = (e_flat[:, None] == jnp.arange(N_EXPERTS, dtype=jnp.int32)[None, :]).astype(jnp.int32)
    incl = jnp.cumsum(onehot, axis=0)
    rank = jnp.sum((incl - onehot) * onehot, axis=1)
    counts = incl[-1]
    tiles_per = (counts + tm - 1) // tm
    tile_end = jnp.cumsum(tiles_per)
    start = (tile_end - tiles_per) * tm
    dest = jnp.sum(onehot * start[None, :], axis=1) + rank
    n_tiles = (2 * mt) // tm + N_EXPERTS
    slot_tok = jnp.zeros((n_tiles * tm,), jnp.int32).at[dest].set(tok)
    slot_gate = jnp.zeros((n_tiles * tm,), F32).at[dest].set(g_flat)
    ti = jnp.arange(n_tiles, dtype=jnp.int32)
    tile_valid = (ti < tile_end[-1]).astype(jnp.int32)
    te = jnp.sum((ti[:, None] >= tile_end[None, :]).astype(jnp.int32), axis=1)
    last_e = jnp.sum((tile_end[-1] - 1 >= tile_end).astype(jnp.int32))
    tile_expert = jnp.where(tile_valid > 0, te, last_e).astype(jnp.int32)
    return slot_tok, slot_gate, tile_expert, tile_valid, dest[:mt], dest[mt:]


def _combine_kernel(x_ref, a_ref, b_ref, o_ref):
    o_ref[...] = x_ref[...] + (a_ref[...] + b_ref[...])


def _combine(x2, a, b):
    m = x2.shape[0]
    tm = _tile(m, 1280, 8)
    spec = pl.BlockSpec((tm, D_MODEL), lambda i: (i, 0))
    return pl.pallas_call(
        _combine_kernel, grid=(m // tm,), in_specs=[spec] * 3, out_specs=spec,
        out_shape=jax.ShapeDtypeStruct((m, D_MODEL), F32),
        compiler_params=_params("parallel"), name="combine",
    )(x2, a, b)


def _final_kernel(x_ref, g_ref, o_ref):
    o_ref[...] = _rms(x_ref[...], g_ref[...])


def _final_norm(x2, g):
    m = x2.shape[0]
    tm = _tile(m, 1280, 8)
    spec = pl.BlockSpec((tm, D_MODEL), lambda i: (i, 0))
    return pl.pallas_call(
        _final_kernel, grid=(m // tm,),
        in_specs=[spec, pl.BlockSpec((1, D_MODEL), lambda i: (0, 0))], out_specs=spec,
        out_shape=jax.ShapeDtypeStruct((m, D_MODEL), F32),
        compiler_params=_params("parallel"), name="final_norm",
    )(x2, g)


def _pack_w_in(w):
    offs = [0]
    for s in IN_SPLITS:
        offs.append(offs[-1] + s)
    seg = [w[:, offs[i]:offs[i + 1]] for i in range(len(IN_SPLITS))]
    qa, ka, va, ra, alr, qb, kb, vb, fb, ga, gb = seg
    wm = jnp.concatenate([va, ra, ga, gb, qa, ka, qb, kb, vb], axis=1).astype(BF)
    ws = jnp.concatenate([fb, alr, jnp.zeros((D_MODEL, LANES - FOX_HEADS - GLA_RANK), w.dtype)], axis=1).astype(BF)
    return wm, ws


def kernel(x_prompt, x_sample, cache_k, cache_v, cache_logf, state_gla, page_table, meta_tokens, norm_mix, w_in,
           w_alpha2, b_alpha, b_f, gla_norm, w_proj_a, w_proj_b, w_out, norm_ffn, w_ffn1, w_ffn3, w_ffn2, w_router,
           w_exp1, w_exp3, w_exp2, norm_final):
    b, seq, _ = x_prompt.shape
    db, n_dec, _ = x_sample.shape
    depth = w_in.shape[0]
    n_pool = cache_k.shape[1]
    t = BLK + seq
    mp, msz = b * t, db * n_dec

    xp = jnp.concatenate([jnp.zeros((b, PAD, D_MODEL), F32),
                          jnp.broadcast_to(meta_tokens[None].astype(F32), (b, N_META, D_MODEL)),
                          x_prompt], axis=1).reshape(mp, D_MODEL)
    xs = x_sample.reshape(msz, D_MODEL)
    cache_k4 = cache_k.reshape(depth, n_pool, BLK, FOX_W)
    cache_v4 = cache_v.reshape(depth, n_pool, BLK, FOX_W)
    cpage = _page_cumsum(cache_logf)
    page_flat = page_table.reshape(-1).astype(jnp.int32)

    outs = {k: [] for k in ("kp", "vp", "lp", "sp", "ks", "vs", "ls", "ss")}
    for l in range(depth):
        wm, ws = _pack_w_in(w_in[l])
        wa = jnp.zeros((LANES, GLA_QK), F32).at[FOX_HEADS:FOX_HEADS + GLA_RANK].set(w_alpha2[l]).astype(BF)
        ba = b_alpha[l].reshape(1, GLA_QK)
        bfp = jnp.zeros((1, LANES), F32).at[0, :FOX_HEADS].set(b_f[l])
        g_mix = norm_mix[l].reshape(1, D_MODEL)
        gn = gla_norm[l].reshape(1, GLA_DV)
        wpa, wpb, wo = w_proj_a[l].astype(BF), w_proj_b[l].astype(BF), w_out[l].astype(BF)
        nf = norm_ffn[l].reshape(1, D_MODEL)
        is_moe = l % 2 == 1
        jx = l // 2

        z_p, la_p, lf_p = _inproj(xp, g_mix, wm, ws, wa, ba, bfp)
        z3 = z_p.reshape(b, t, NZ)
        c_rows = _fox_cumsum(lf_p.reshape(b, t, LANES))
        ck4 = c_rows[:, :FOX_HEADS, :].reshape(b, FOX_HEADS // 2, 2, t)
        ob_p = _fox_prompt(z3, ck4)
        og_p, s_p = _gla_prompt(z3, la_p.reshape(b, t, GLA_QK))
        z_s, la_s, lf_s = _inproj(xs, g_mix, wm, ws, wa, ba, bfp)
        zs3 = z_s.reshape(db, n_dec, NZ)
        og_s, s_s = _gla_sample(zs3, la_s.reshape(db, n_dec, GLA_QK), state_gla[l])
        kn3 = zs3[:, :, KB:KB + FOX_W]
        vn3 = zs3[:, :, VB:VB + FOX_W]
        ob_s = _fox_sample(page_flat, cache_k4, cache_v4, cpage, l, zs3, kn3, vn3,
                           lf_s.reshape(db, n_dec, LANES))

        outs["kp"].append(z3[:, PAD:, KB:KB + FOX_W].reshape(b, N_META + seq, FOX_HEADS, FOX_HD))
        outs["vp"].append(z3[:, PAD:, VB:VB + FOX_W].reshape(b, N_META + seq, FOX_HEADS, FOX_HD))
        outs["lp"].append(lf_p.reshape(b, t, LANES)[:, PAD:, :FOX_HEADS])
        outs["sp"].append(s_p)
        outs["ks"].append(kn3.reshape(db, n_dec, FOX_HEADS, FOX_HD))
        outs["vs"].append(vn3.reshape(db, n_dec, FOX_HEADS, FOX_HD))
        outs["ls"].append(lf_s.reshape(db, n_dec, LANES)[:, :, :FOX_HEADS])
        outs["ss"].append(s_s)

        if not is_moe:
            xp, h_p = _merge(xp, z_p, og_p.reshape(mp, GLA_V), ob_p.reshape(mp, FOX_W), gn, wpa, wpb, wo, nf)
            xs, h_s = _merge(xs, z_s, og_s.reshape(msz, GLA_V), ob_s.reshape(msz, FOX_W), gn, wpa, wpb, wo, nf)
            w1, w3, w2 = w_ffn1[jx].astype(BF), w_ffn3[jx].astype(BF), w_ffn2[jx].astype(BF)
            xp = _ffn(h_p, xp, w1, w3, w2)
            xs = _ffn(h_s, xs, w1, w3, w2)
        else:
            wr = jnp.zeros((D_MODEL, LANES), F32).at[:, :N_EXPERTS].set(w_router[jx])
            wr1 = wr.astype(BF)
            wr2 = (wr - wr1.astype(F32)).astype(BF)
            wr = jnp.stack([wr1, wr2])
            xp, h_p, idx_p, gate_p = _merge(xp, z_p, og_p.reshape(mp, GLA_V), ob_p.reshape(mp, FOX_W),
                                            gn, wpa, wpb, wo, nf, wr)
            xs, h_s, idx_s, gate_s = _merge(xs, z_s, og_s.reshape(msz, GLA_V), ob_s.reshape(msz, FOX_W),
                                            gn, wpa, wpb, wo, nf, wr)
            h_all = jnp.concatenate([h_p, h_s], axis=0)
            idx = jnp.concatenate([idx_p[:, :2], idx_s[:, :2]], axis=0)
            gate = jnp.concatenate([gate_p[:, :2], gate_s[:, :2]], axis=0)
            tm = 1024 if 2 * (mp + msz) >= 8 * 1024 else 256
            slot_tok, slot_gate, tile_e, tile_v, d1, d2 = _moe_route(idx, gate, tm)
            hs = jnp.take(h_all, slot_tok, axis=0)
            ys = _moe_experts(tile_e, tile_v, hs, slot_gate[:, None],
                              w_exp1[jx].astype(BF), w_exp3[jx].astype(BF), w_exp2[jx].astype(BF), tm)
            x_all = _combine(jnp.concatenate([xp, xs], axis=0), jnp.take(ys, d1, axis=0), jnp.take(ys, d2, axis=0))
            xp, xs = x_all[:mp], x_all[mp:]

    g_fin = norm_final.reshape(1, D_MODEL)
    y_prompt = _final_norm(xp, g_fin).reshape(b, t, D_MODEL)[:, BLK:]
    y_sample = _final_norm(xs, g_fin).reshape(db, n_dec, D_MODEL)
    st = jnp.stack
    return (y_prompt, y_sample, st(outs["kp"]), st(outs["vp"]), st(outs["lp"]), st(outs["sp"]),
            st(outs["ks"]), st(outs["vs"]), st(outs["ls"]), st(outs["ss"]))
```

```python
import functools

import jax
import jax.numpy as jnp
from jax import lax
from jax.experimental import pallas as pl
from jax.experimental.pallas import tpu as pltpu

BF = jnp.bfloat16
F32 = jnp.float32

D_MODEL = 1024
N_META = 16
BLK = 128
GLA_HEADS = 4
GLA_DK = 128
GLA_DV = 256
GLA_RANK = 16
GLA_TAU = 16.0
GLA_SUB = 16
FOX_HEADS = 8
FOX_HD = 64
N_EXPERTS = 8
RMS_EPS = 1e-6
NEG = -1e30
LANES = 128

GLA_QK = GLA_HEADS * GLA_DK
GLA_V = GLA_HEADS * GLA_DV
FOX_W = FOX_HEADS * FOX_HD
VA, RA, GA, GB = 0, 1024, 2048, 3072
QA, KA = 4096, 4608
QB, KB, VB = 5120, 5632, 6144
NZ_PROMPT = 5632
NZ = 6656
IN_SPLITS = (GLA_QK, GLA_QK, GLA_V, GLA_V, GLA_RANK, FOX_W, FOX_W, FOX_W, FOX_HEADS, D_MODEL, D_MODEL)

VMEM_LIMIT = 56 * 1024 * 1024


def _params(*sem):
    return pltpu.CompilerParams(dimension_semantics=sem, vmem_limit_bytes=VMEM_LIMIT)


def _tile(n, target, mult):
    best = None
    for t in range(mult, min(n, target) + 1, mult):
        if n % t == 0:
            best = t
    return best if best is not None else n


def _dot(a, b):
    return jnp.dot(a, b, preferred_element_type=F32)


def _dot_nt(a, b):
    return lax.dot_general(a, b, (((1,), (1,)), ((), ())), preferred_element_type=F32)


def _split2(x):
    hi = x.astype(BF)
    return hi, (x - hi.astype(F32)).astype(BF)


def _dot3(a, b, nt=False):
    dot = _dot_nt if nt else _dot
    ah, al = _split2(a)
    bh, bl = _split2(b)
    return (dot(al, bh) + dot(ah, bl)) + dot(ah, bh)


def _mm(a, w, precise):
    return _dot3(a, w) if precise else _dot(a.astype(BF), w)


def _split3(x):
    h1 = x.astype(BF)
    r1 = x - h1.astype(F32)
    h2 = r1.astype(BF)
    r2 = r1 - h2.astype(F32)
    return h1, h2, r2.astype(BF)


def _dot_exact_r(x, m):
    h1, h2, h3 = _split3(x)
    return (_dot(h3, m) + _dot(h2, m)) + _dot(h1, m)


def _dot_exact_l(m, x):
    h1, h2, h3 = _split3(x)
    return (_dot(m, h3) + _dot(m, h2)) + _dot(m, h1)


def _logsig(x):
    return jnp.minimum(x, 0.0) - jnp.log1p(jnp.exp(-jnp.abs(x)))


def _rms(x, g):
    ms = jnp.mean(x * x, axis=-1, keepdims=True)
    return x * lax.rsqrt(ms + RMS_EPS) * g


def _inproj_kernel(*refs, head_major):
    if head_major:
        (x_ref, g_ref, wm_ref, ws_ref, wa_ref, ba_ref, bf_ref, wkv_ref, wst_ref, bfc_ref,
         z_ref, la_ref, kt_ref, vt_ref, lft_ref, h_scr) = refs
    else:
        x_ref, g_ref, wm_ref, ws_ref, wa_ref, ba_ref, bf_ref, z_ref, la_ref, lf_ref, h_scr = refs

    precise = not head_major

    @pl.when(pl.program_id(1) == 0)
    def _():
        hf = _rms(x_ref[...], g_ref[...])
        h, h_lo = _split2(hf)
        h_scr[0] = h
        h_scr[1] = h_lo
        s = _mm(hf, ws_ref[...], precise)
        a = _mm(s, wa_ref[...], precise) + ba_ref[...]
        la_ref[...] = _logsig(a) * (1.0 / GLA_TAU)
        if head_major:
            tm = h.shape[0]
            kv = _dot_nt(wkv_ref[...], h)
            kt_ref[0] = kv[:FOX_W].reshape(FOX_HEADS, FOX_HD, tm)
            vt_ref[0] = kv[FOX_W:].reshape(FOX_HEADS, FOX_HD, tm)
            st = _dot_nt(wst_ref[...], h)
            lft_ref[0] = _logsig(st[:FOX_HEADS] + bfc_ref[...])
        else:
            lf_ref[...] = _logsig(s + bf_ref[...])

    if precise:
        wh, wl = _split2(wm_ref[...])
        z_ref[...] = (_dot(h_scr[1], wh) + _dot(h_scr[0], wl)) + _dot(h_scr[0], wh)
    else:
        z_ref[...] = _dot(h_scr[0], wm_ref[...])


def _inproj(x2, g, wm, ws, wa, ba, bfp, head_major=None):
    m = x2.shape[0]
    nz = wm.shape[1]
    const = lambda shape: pl.BlockSpec(shape, lambda i, j: (0,) * len(shape))
    if head_major is not None:
        b, wkv, wst, bfc = head_major
        t = m // b
        tm = _tile(t, 640, BLK)
        nt = t // tm
    else:
        tm = _tile(m, 1280, 8)
    tn = _tile(nz, 1664, LANES)
    in_specs = [
        pl.BlockSpec((tm, D_MODEL), lambda i, j: (i, 0)),
        const((1, D_MODEL)),
        pl.BlockSpec((D_MODEL, tn), lambda i, j: (0, j)),
        const((D_MODEL, LANES)), const((LANES, GLA_QK)), const((1, GLA_QK)), const((1, LANES)),
    ]
    args = [x2, g, wm, ws, wa, ba, bfp]
    out_specs = [pl.BlockSpec((tm, tn), lambda i, j: (i, j)),
                 pl.BlockSpec((tm, GLA_QK), lambda i, j: (i, 0))]
    out_shape = [jax.ShapeDtypeStruct((m, nz), F32), jax.ShapeDtypeStruct((m, GLA_QK), F32)]
    if head_major is not None:
        in_specs += [const((2 * FOX_W, D_MODEL)), const((LANES, D_MODEL)), const((FOX_HEADS, 1))]
        args += [wkv, wst, bfc]
        hm = pl.BlockSpec((1, FOX_HEADS, FOX_HD, tm), lambda i, j: (i // nt, 0, 0, i % nt))
        out_specs += [hm, hm, pl.BlockSpec((1, FOX_HEADS, tm), lambda i, j: (i // nt, 0, i % nt))]
        out_shape += [jax.ShapeDtypeStruct((b, FOX_HEADS, FOX_HD, t), F32)] * 2
        out_shape += [jax.ShapeDtypeStruct((b, FOX_HEADS, t), F32)]
    else:
        out_specs.append(pl.BlockSpec((tm, LANES), lambda i, j: (i, 0)))
        out_shape.append(jax.ShapeDtypeStruct((m, LANES), F32))
    return pl.pallas_call(
        functools.partial(_inproj_kernel, head_major=head_major is not None),
        grid=(m // tm, nz // tn),
        in_specs=in_specs, out_specs=out_specs, out_shape=out_shape,
        scratch_shapes=[pltpu.VMEM((2, tm, D_MODEL), BF)],
        compiler_params=_params("parallel", "arbitrary"),
        name="inproj",
    )(*args)


def _cumsum_kernel(x_ref, c_ref, carry, *, nsub, running):
    if running:
        @pl.when(pl.program_id(1) == 0)
        def _():
            carry[...] = jnp.zeros_like(carry)

    r = lax.broadcasted_iota(jnp.int32, (BLK, BLK), 0)
    c = lax.broadcasted_iota(jnp.int32, (BLK, BLK), 1)
    upper = (r <= c).astype(BF)
    for u in range(nsub):
        cb = _dot_exact_r(x_ref[0, :, u * BLK:(u + 1) * BLK], upper)
        if running:
            cb = cb + carry[...]
            carry[...] = cb[:, BLK - 1:BLK]
        c_ref[0, :, u * BLK:(u + 1) * BLK] = cb


def _cumsum_lanes(x3, running):
    a, rows, t = x3.shape
    tb = _tile(t, 640, BLK)
    return pl.pallas_call(
        functools.partial(_cumsum_kernel, nsub=tb // BLK, running=running),
        grid=(a, t // tb),
        in_specs=[pl.BlockSpec((1, rows, tb), lambda i, j: (i, 0, j))],
        out_specs=pl.BlockSpec((1, rows, tb), lambda i, j: (i, 0, j)),
        out_shape=jax.ShapeDtypeStruct(x3.shape, F32),
        scratch_shapes=[pltpu.VMEM((rows, 1), F32)],
        compiler_params=_params("parallel", "arbitrary"),
        name="cumsum_lanes",
    )(x3)


def _fox_prompt_kernel(q_ref, k_ref, v_ref, ck_ref, o_ref, ka, va, m_scr, acc_scr, *, tq):
    qi = pl.program_id(2)
    t = ka.shape[2]

    @pl.when(qi == 0)
    def _():
        row = lax.broadcasted_iota(jnp.int32, (FOX_HD, t), 0)
        for h in range(2):
            c1, c2, c3 = _split3(-ck_ref[0, 0, h:h + 1, :])
            extra = jnp.where(row == 0, c1.astype(F32), jnp.where(row == 1, c2.astype(F32),
                              jnp.where(row == 2, c3.astype(F32), 0.0)))
            ka[h] = jnp.concatenate([k_ref[0, h], extra], axis=0).astype(BF)
            va[h] = jnp.concatenate([v_ref[0, h], jnp.where(row == 0, 1.0, 0.0)], axis=0).astype(BF)

    lane = lax.broadcasted_iota(jnp.int32, (1, LANES), 1)
    ones3 = jnp.where((lane >= FOX_HD) & (lane < FOX_HD + 3), 1.0, 0.0)
    q2 = q_ref[0] * (FOX_HD ** -0.5)
    qh = (jnp.where(lane < FOX_HD, q2, ones3).astype(BF),
          jnp.where(lane < FOX_HD, pltpu.roll(q2, FOX_HD, 1), ones3).astype(BF))
    m_scr[...] = jnp.full(m_scr.shape, NEG, F32)
    acc_scr[...] = jnp.zeros(acc_scr.shape, F32)

    def logits(j):
        start = pl.multiple_of(j * tq, tq)
        return tuple(_dot(qh[h], ka[h, :, pl.ds(start, tq)]) for h in range(2))

    def attend(j, ss, causal):
        start = pl.multiple_of(j * tq, tq)
        for h in range(2):
            s = ss[h]
            if causal:
                ri = lax.broadcasted_iota(jnp.int32, (tq, tq), 0)
                ci = lax.broadcasted_iota(jnp.int32, (tq, tq), 1)
                s = jnp.where(ci <= ri, s, NEG)
            m_old = m_scr[h]
            m_new = jnp.maximum(m_old, jnp.max(s, axis=-1, keepdims=True))
            p = jnp.exp(s - m_new).astype(BF)
            acc_scr[h] = jnp.exp(m_old - m_new) * acc_scr[h] + _dot_nt(p, va[h, :, pl.ds(start, tq)])
            m_scr[h] = m_new

    def tiles(*js_causal):
        ss = [logits(j) for j, _ in js_causal]
        for (j, causal), s in zip(js_causal, ss):
            attend(j, s, causal)

    def body(j2, carry):
        tiles((2 * j2, False), (2 * j2 + 1, False))
        return carry

    lax.fori_loop(0, qi // 2, body, 0)

    @pl.when((qi & 1) == 1)
    def _():
        tiles((qi - 1, False), (qi, True))

    @pl.when((qi & 1) == 0)
    def _():
        tiles((qi, True))
    o0 = acc_scr[0]
    o1 = acc_scr[1]
    o0 = o0 / o0[:, FOX_HD:FOX_HD + 1]
    o1 = o1 / o1[:, FOX_HD:FOX_HD + 1]
    o_ref[0] = jnp.where(lane < FOX_HD, o0, pltpu.roll(o1, FOX_HD, 1))


def _fox_prompt(z3, kt, vt, ck4):
    b, t, _ = z3.shape
    tq = _tile(t, 640, BLK)
    npair = FOX_HEADS // 2
    kv_spec = pl.BlockSpec((1, 2, FOX_HD, t), lambda i, hp, qi: (i, hp, 0, 0))
    return pl.pallas_call(
        functools.partial(_fox_prompt_kernel, tq=tq),
        grid=(b, npair, t // tq),
        in_specs=[
            pl.BlockSpec((1, tq, LANES), lambda i, hp, qi: (i, qi, QB // LANES + hp)),
            kv_spec, kv_spec,
            pl.BlockSpec((1, 1, 2, t), lambda i, hp, qi: (i, hp, 0, 0)),
        ],
        out_specs=pl.BlockSpec((1, tq, LANES), lambda i, hp, qi: (i, qi, hp)),
        out_shape=jax.ShapeDtypeStruct((b, t, FOX_W), F32),
        scratch_shapes=[
            pltpu.VMEM((2, 2 * FOX_HD, t), BF), pltpu.VMEM((2, 2 * FOX_HD, t), BF),
            pltpu.VMEM((2, tq, 1), F32), pltpu.VMEM((2, tq, LANES), F32),
        ],
        compiler_params=_params("parallel", "parallel", "arbitrary"),
        name="fox_prompt",
    )(z3, kt, vt, ck4)


def _gla_masks():
    r = jnp.arange(BLK)[:, None]
    c = jnp.arange(BLK)[None, :]
    masks = [(c < GLA_SUB) & ((r % GLA_SUB) + c < GLA_SUB)]
    m = BLK // 2
    while m >= GLA_SUB:
        masks.append((r // (2 * m) == c // (2 * m)) & (r % (2 * m) >= m) & (c % (2 * m) < m))
        m //= 2
    k = jnp.arange(GLA_SUB * GLA_DK)[:, None]
    collect = (k // GLA_DK == c).astype(BF)
    return jnp.stack(masks).astype(F32), collect


def _gla_chunk_math(qs, ks, vs, las, s0s, msk_ref, col_ref, pq, pc, prod):
    heads = range(len(qs))
    ri = lax.broadcasted_iota(jnp.int32, (BLK, BLK), 0)
    ci = lax.broadcasted_iota(jnp.int32, (BLK, BLK), 1)
    lower = (ri >= ci).astype(BF)
    cums = [_dot_exact_l(lower, las[h]) for h in heads]
    for h in heads:
        pq[h, 0:BLK, :] = qs[h]
        pc[h, 0:BLK, :] = cums[h]
        pq[h, BLK:, :] = jnp.zeros((GLA_SUB, GLA_DK), F32)
        pc[h, BLK:, :] = jnp.zeros((GLA_SUB, GLA_DK), F32)
    for h in heads:
        for d in range(GLA_SUB):
            e = jnp.exp(jnp.minimum(pc[h, d:d + BLK, :] - cums[h], 0.0))
            prod[h, :, d * GLA_DK:(d + 1) * GLA_DK] = (pq[h, d:d + BLK, :] * ks[h] * e).astype(BF)
    bands = [_dot(prod[h], col_ref[...]) for h in heads]
    a = [pltpu.roll(jnp.where(msk_ref[0] > 0.5, bands[h], 0.0), 0, 1, stride=1, stride_axis=0).T
         for h in heads]
    m = BLK // 2
    lvl = 1
    while m >= GLA_SUB:
        nb = BLK // (2 * m)
        up = (ri & (2 * m - 1)) >= m
        for h in heads:
            cum = cums[h]
            pieces = [jnp.broadcast_to(cum[2 * m * u + m - 1:2 * m * u + m, :], (2 * m, GLA_DK)) for u in range(nb)]
            bnd = pieces[0] if nb == 1 else jnp.concatenate(pieces, axis=0)
            dq = cum - bnd
            qt = jnp.where(up, qs[h] * jnp.exp(jnp.minimum(dq, 0.0)), 0.0).astype(BF)
            kt = jnp.where(up, 0.0, ks[h] * jnp.exp(jnp.minimum(-dq, 0.0))).astype(BF)
            a[h] = jnp.where(msk_ref[lvl] > 0.5, _dot_nt(qt, kt), a[h])
        m //= 2
        lvl += 1
    outs = []
    for h in heads:
        cum = cums[h]
        vb = vs[h].astype(BF)
        o = _dot(a[h].astype(BF), vb) + _dot((qs[h] * jnp.exp(cum)).astype(BF), s0s[h].astype(BF))
        last = cum[BLK - 1:BLK, :]
        kd_t = (ks[h] * jnp.exp(last - cum)).T.astype(BF)
        last_col = jnp.sum(jnp.where(ri == ci, last, 0.0), axis=1, keepdims=True)
        outs.append((o, jnp.exp(last_col) * s0s[h] + _dot(kd_t, vb)))
    return outs


def _gla_prompt_kernel(q_ref, k_ref, v_ref, la_ref, msk_ref, col_ref, o_ref, s_ref, s_scr, pq, pc, prod, *, n_valid):
    c = pl.program_id(1)

    @pl.when(c == 0)
    def _():
        s_scr[...] = jnp.zeros_like(s_scr)

    real = (c * BLK + lax.broadcasted_iota(jnp.int32, (BLK, GLA_DK), 0)) < n_valid
    dk = lambda h: slice(h * GLA_DK, (h + 1) * GLA_DK)
    dv = lambda h: slice(h * GLA_DV, (h + 1) * GLA_DV)
    heads = range(GLA_HEADS)
    qs = [q_ref[0, :, dk(h)] * (GLA_DK ** -0.5) for h in heads]
    ks = [jnp.where(real, k_ref[0, :, dk(h)], 0.0) for h in heads]
    las = [jnp.where(real, la_ref[0, :, dk(h)], 0.0) for h in heads]
    vs = [v_ref[0, :, dv(h)] for h in heads]
    outs = _gla_chunk_math(qs, ks, vs, las, [s_scr[h] for h in heads], msk_ref, col_ref, pq, pc, prod)
    for h in heads:
        o, s1 = outs[h]
        o_ref[0, :, dv(h)] = o
        s_scr[h] = s1
        s_ref[0, h] = s1


def _gla_prompt(z3, la3, n_valid):
    b, t, _ = z3.shape
    masks, collect = _gla_masks()
    return pl.pallas_call(
        functools.partial(_gla_prompt_kernel, n_valid=n_valid),
        grid=(b, t // BLK),
        in_specs=[
            pl.BlockSpec((1, BLK, GLA_QK), lambda i, c: (i, c, QA // GLA_QK)),
            pl.BlockSpec((1, BLK, GLA_QK), lambda i, c: (i, c, KA // GLA_QK)),
            pl.BlockSpec((1, BLK, GLA_V), lambda i, c: (i, c, VA // GLA_V)),
            pl.BlockSpec((1, BLK, GLA_QK), lambda i, c: (i, c, 0)),
            pl.BlockSpec(masks.shape, lambda i, c: (0, 0, 0)),
            pl.BlockSpec(collect.shape, lambda i, c: (0, 0)),
        ],
        out_specs=[
            pl.BlockSpec((1, BLK, GLA_V), lambda i, c: (i, c, 0)),
            pl.BlockSpec((1, GLA_HEADS, GLA_DK, GLA_DV), lambda i, c: (i, 0, 0, 0)),
        ],
        out_shape=[
            jax.ShapeDtypeStruct((b, t, GLA_V), F32),
            jax.ShapeDtypeStruct((b, GLA_HEADS, GLA_DK, GLA_DV), F32),
        ],
        scratch_shapes=[pltpu.VMEM((GLA_HEADS, GLA_DK, GLA_DV), F32),
                        pltpu.VMEM((GLA_HEADS, BLK + GLA_SUB, GLA_DK), F32),
                        pltpu.VMEM((GLA_HEADS, BLK + GLA_SUB, GLA_DK), F32),
                        pltpu.VMEM((GLA_HEADS, BLK, GLA_SUB * GLA_DK), BF)],
        compiler_params=_params("parallel", "arbitrary"),
        name="gla_prompt",
    )(z3, z3, z3, la3, masks, collect)


def _gla_sample_kernel(q_ref, k_ref, v_ref, la_ref, s0_ref, o_ref, s1_ref, pq, pk, pc, pv, *, n):
    ri = lax.broadcasted_iota(jnp.int32, (n, GLA_DK), 0)
    for h in range(GLA_HEADS):
        q = q_ref[0, :, h * GLA_DK:(h + 1) * GLA_DK] * (GLA_DK ** -0.5)
        k = k_ref[0, :, h * GLA_DK:(h + 1) * GLA_DK]
        la = la_ref[0, :, h * GLA_DK:(h + 1) * GLA_DK]
        v = v_ref[0, :, h * GLA_DV:(h + 1) * GLA_DV]
        s0 = s0_ref[0, h]
        cum = jnp.zeros_like(la)
        for j in range(n):
            cum = cum + jnp.where(ri >= j, la[j:j + 1, :], 0.0)
        o = jnp.zeros((n, GLA_DV), F32)
        for s in range(n):
            w = q * jnp.exp(jnp.minimum(cum - cum[s:s + 1, :], 0.0)) * k[s:s + 1, :]
            sc = jnp.sum(w, axis=-1, keepdims=True)
            o = o + jnp.where(ri[:, :1] >= s, sc, 0.0) * v[s:s + 1, :]
        for scr, val in ((pq, q * jnp.exp(cum)), (pk, k), (pc, cum)):
            scr[...] = jnp.zeros_like(scr)
            scr[0:n, :] = val
        pv[...] = jnp.zeros_like(pv)
        pv[0:n, :] = v
        o_inter = _dot3(pq[...], s0)
        o_ref[0, :, h * GLA_DV:(h + 1) * GLA_DV] = o + o_inter[0:n, :]
        cum_t = pc[...].T
        k_t = pk[...].T
        last = cum_t[:, n - 1:n]
        kd = k_t * jnp.exp(jnp.minimum(last - cum_t, 0.0))
        s1_ref[0, h] = jnp.exp(last) * s0 + _dot3(kd, pv[...])


def _gla_sample(z3, la3, state):
    db, n, _ = z3.shape
    return pl.pallas_call(
        functools.partial(_gla_sample_kernel, n=n),
        grid=(db,),
        in_specs=[
            pl.BlockSpec((1, n, GLA_QK), lambda i: (i, 0, QA // GLA_QK)),
            pl.BlockSpec((1, n, GLA_QK), lambda i: (i, 0, KA // GLA_QK)),
            pl.BlockSpec((1, n, GLA_V), lambda i: (i, 0, VA // GLA_V)),
            pl.BlockSpec((1, n, GLA_QK), lambda i: (i, 0, 0)),
            pl.BlockSpec((1, GLA_HEADS, GLA_DK, GLA_DV), lambda i: (i, 0, 0, 0)),
        ],
        out_specs=[
            pl.BlockSpec((1, n, GLA_V), lambda i: (i, 0, 0)),
            pl.BlockSpec((1, GLA_HEADS, GLA_DK, GLA_DV), lambda i: (i, 0, 0, 0)),
        ],
        out_shape=[
            jax.ShapeDtypeStruct((db, n, GLA_V), F32),
            jax.ShapeDtypeStruct(state.shape, F32),
        ],
        scratch_shapes=[pltpu.VMEM((BLK, GLA_DK), F32), pltpu.VMEM((BLK, GLA_DK), F32),
                        pltpu.VMEM((BLK, GLA_DK), F32), pltpu.VMEM((BLK, GLA_DV), F32)],
        compiler_params=_params("parallel"),
        name="gla_sample",
    )(z3, z3, z3, la3, state)


def _fox_sample_kernel(pt_ref, *refs, pps, n):
    del pt_ref
    k_refs = refs[0:pps]
    v_refs = refs[pps:2 * pps]
    c_refs = refs[2 * pps:3 * pps]
    q_ref, kn_ref, vn_ref, lfn_ref, o_ref, qbd, carry, m_scr, l_scr, acc_scr = refs[3 * pps:]
    j = pl.program_id(1)
    rows = n * FOX_HEADS
    rr = lax.broadcasted_iota(jnp.int32, (rows, FOX_W), 0)
    cc = lax.broadcasted_iota(jnp.int32, (rows, FOX_W), 1)
    own = (cc >> 6) == (rr & (FOX_HEADS - 1))

    @pl.when(j == 0)
    def _():
        qrep = jnp.concatenate([jnp.broadcast_to(q_ref[0, t:t + 1, :], (FOX_HEADS, FOX_W)) for t in range(n)], axis=0)
        qbd[...] = jnp.where(own, qrep * (FOX_HD ** -0.5), 0.0)
        carry[...] = jnp.zeros_like(carry)
        m_scr[...] = jnp.full(m_scr.shape, NEG, F32)
        l_scr[...] = jnp.zeros_like(l_scr)
        acc_scr[...] = jnp.zeros_like(acc_scr)

    qh, ql = _split2(qbd[...])
    run = carry[...]
    ss = []
    for r in range(pps):
        cg = run + c_refs[r][0, 0]
        run = cg[:, BLK - 1:BLK]
        kh, kl = _split2(k_refs[r][0, 0])
        ss.append(((_dot(ql, kh) + _dot(qh, kl)) + _dot(qh, kh)) - jnp.concatenate([cg] * n, axis=0))
    carry[...] = run
    s = jnp.concatenate(ss, axis=1)
    m_old = m_scr[...]
    m_new = jnp.maximum(m_old, jnp.max(s, axis=-1, keepdims=True))
    p = jnp.exp(s - m_new)
    alpha = jnp.exp(m_old - m_new)
    l_scr[...] = alpha * l_scr[...] + jnp.sum(p, axis=-1, keepdims=True)
    ph, pl_ = _split2(p)
    pv = None
    for r in range(pps):
        cols = slice(r * BLK, (r + 1) * BLK)
        vh, vl = _split2(v_refs[r][0, 0])
        t3 = (_dot_nt(pl_[:, cols], vh) + _dot_nt(ph[:, cols], vl)) + _dot_nt(ph[:, cols], vh)
        pv = t3 if pv is None else pv + t3
    acc_scr[...] = alpha * acc_scr[...] + pv
    m_scr[...] = m_new

    @pl.when(j == pl.num_programs(1) - 1)
    def _():
        lfn = lfn_ref[0]
        ri = lax.broadcasted_iota(jnp.int32, (n, LANES), 0)
        cn = jnp.zeros_like(lfn)
        for i in range(n):
            cn = cn + jnp.where(ri >= i, lfn[i:i + 1, :], 0.0)
        er = lax.broadcasted_iota(jnp.int32, (rows, LANES), 0)
        ec = lax.broadcasted_iota(jnp.int32, (rows, LANES), 1)
        sel = (er & (FOX_HEADS - 1)) == ec
        tot = jnp.concatenate([carry[...]] * n, axis=0)
        qf = qbd[...]
        tok = rr[:, :1] >> 3
        m_run, l_run, acc = m_scr[...], l_scr[...], acc_scr[...]
        for i in range(n):
            ci = jnp.sum(jnp.where(sel, cn[i:i + 1, :], 0.0), axis=-1, keepdims=True)
            s = jnp.sum(qf * kn_ref[0, i:i + 1, :], axis=-1, keepdims=True) - (tot + ci)
            s = jnp.where(tok >= i, s, NEG)
            m_new = jnp.maximum(m_run, s)
            p = jnp.exp(s - m_new)
            alpha = jnp.exp(m_run - m_new)
            l_run = alpha * l_run + p
            acc = alpha * acc + p * vn_ref[0, i:i + 1, :]
            m_run = m_new
        o_full = jnp.where(own, acc / l_run, 0.0)
        o_ref[0] = jnp.sum(o_full.reshape(n, FOX_HEADS, FOX_W), axis=1)


def _fox_sample(page_flat, cache_kt, cache_vt, cpage, layer, q3, kn3, vn3, lfn3):
    db, n, _ = q3.shape
    n_pages = page_flat.shape[0] // db
    pps = _tile(n_pages, 8, 1)
    rows = n * FOX_HEADS

    def page_map(r):
        return lambda i, j, pt: (layer, pt[i * n_pages + j * pps + r], 0, 0)

    def kv_specs():
        return [pl.BlockSpec((1, 1, FOX_W, BLK), page_map(r)) for r in range(pps)]

    c_specs = [pl.BlockSpec((1, 1, FOX_HEADS, BLK), page_map(r)) for r in range(pps)]
    grid_spec = pltpu.PrefetchScalarGridSpec(
        num_scalar_prefetch=1,
        grid=(db, n_pages // pps),
        in_specs=kv_specs() + kv_specs() + c_specs + [
            pl.BlockSpec((1, n, FOX_W), lambda i, j, pt: (i, 0, QB // FOX_W)),
            pl.BlockSpec((1, n, FOX_W), lambda i, j, pt: (i, 0, 0)),
            pl.BlockSpec((1, n, FOX_W), lambda i, j, pt: (i, 0, 0)),
            pl.BlockSpec((1, n, LANES), lambda i, j, pt: (i, 0, 0)),
        ],
        out_specs=pl.BlockSpec((1, n, FOX_W), lambda i, j, pt: (i, 0, 0)),
        scratch_shapes=[
            pltpu.VMEM((rows, FOX_W), F32), pltpu.VMEM((FOX_HEADS, 1), F32),
            pltpu.VMEM((rows, 1), F32), pltpu.VMEM((rows, 1), F32), pltpu.VMEM((rows, FOX_W), F32),
        ],
    )
    return pl.pallas_call(
        functools.partial(_fox_sample_kernel, pps=pps, n=n),
        grid_spec=grid_spec,
        out_shape=jax.ShapeDtypeStruct((db, n, FOX_W), F32),
        compiler_params=_params("parallel", "arbitrary"),
        name="fox_sample",
    )(page_flat, *([cache_kt] * pps), *([cache_vt] * pps), *([cpage] * pps), q3, kn3, vn3, lfn3)


def _merge_kernel(*refs, moe, precise):
    (x_ref, og_ref, r_ref, ob_ref, ga_ref, gb_ref, gn_ref, wpa_ref, wpb_ref, wo_ref, nf_ref) = refs[:11]
    if moe:
        wr_ref, xo_ref, h_ref, idx_ref, gate_ref = refs[11:]
    else:
        xo_ref, h_ref = refs[11:]
    og = og_ref[...]
    gn = gn_ref[...]
    oa = jnp.concatenate([_rms(og[:, h * GLA_DV:(h + 1) * GLA_DV], gn) for h in range(GLA_HEADS)], axis=-1)
    oa = oa * jax.nn.silu(r_ref[...])
    pa = _mm(oa, wpa_ref[...], precise)
    pb = _mm(ob_ref[...], wpb_ref[...], precise)
    y = jax.nn.sigmoid(ga_ref[...]) * pa + jax.nn.sigmoid(gb_ref[...]) * pb
    xn = x_ref[...] + _mm(y, wo_ref[...], precise)
    xo_ref[...] = xn
    h = _rms(xn, nf_ref[...])
    if moe:
        h_ref[...] = h
        h1 = h.astype(BF)
        h2 = (h - h1.astype(F32)).astype(BF)
        w1 = wr_ref[0]
        w2 = wr_ref[1]
        logits = (_dot(h2, w1) + _dot(h1, w2)) + _dot(h1, w1)
        lane = lax.broadcasted_iota(jnp.int32, logits.shape, 1)
        logits = jnp.where(lane < N_EXPERTS, logits, -jnp.inf)
        v1 = jnp.max(logits, axis=-1, keepdims=True)
        i1 = jnp.min(jnp.where(logits == v1, lane, LANES), axis=-1, keepdims=True)
        rest = jnp.where(lane == i1, -jnp.inf, logits)
        v2 = jnp.max(rest, axis=-1, keepdims=True)
        i2 = jnp.min(jnp.where(rest == v2, lane, LANES), axis=-1, keepdims=True)
        e2 = jnp.exp(v2 - v1)
        g1 = 1.0 / (1.0 + e2)
        g2 = e2 / (1.0 + e2)
        idx_ref[...] = jnp.where(lane == 0, i1, jnp.where(lane == 1, i2, 0))
        gate_ref[...] = jnp.where(lane == 0, g1, jnp.where(lane == 1, g2, 0.0))
    else:
        h_ref[...] = h.astype(h_ref.dtype)


def _merge(x2, z, og, ob, gn, wpa, wpb, wo, nf, wr=None):
    m = x2.shape[0]
    moe = wr is not None
    precise = wpa.dtype == F32
    tm = _tile(m, 320, 8)
    cb = D_MODEL
    row = lambda c: pl.BlockSpec((tm, cb), lambda i: (i, c))
    full = lambda a: pl.BlockSpec(a.shape, lambda i: (0,) * a.ndim)
    in_specs = [row(0), row(0), row(RA // cb), pl.BlockSpec((tm, FOX_W), lambda i: (i, 0)),
                row(GA // cb), row(GB // cb),
                full(gn), full(wpa), full(wpb), full(wo), full(nf)]
    args = [x2, og, z, ob, z, z, gn, wpa, wpb, wo, nf]
    out_specs = [row(0), row(0)]
    out_shape = [jax.ShapeDtypeStruct((m, D_MODEL), F32), jax.ShapeDtypeStruct((m, D_MODEL), F32 if (moe or precise) else BF)]
    if moe:
        in_specs.append(full(wr))
        args.append(wr)
        out_specs += [pl.BlockSpec((tm, LANES), lambda i: (i, 0))] * 2
        out_shape += [jax.ShapeDtypeStruct((m, LANES), jnp.int32), jax.ShapeDtypeStruct((m, LANES), F32)]
    return pl.pallas_call(
        functools.partial(_merge_kernel, moe=moe, precise=precise),
        grid=(m // tm,),
        in_specs=in_specs, out_specs=out_specs, out_shape=out_shape,
        compiler_params=_params("parallel"),
        name="merge",
    )(*args)


def _ffn_kernel(h_ref, x_ref, w1_ref, w3_ref, w2_ref, o_ref, acc, *, precise):
    j = pl.program_id(1)

    @pl.when(j == 0)
    def _():
        acc[...] = jnp.zeros_like(acc)

    h = h_ref[...]
    u = jax.nn.silu(_mm(h, w1_ref[...], precise)) * _mm(h, w3_ref[...], precise)
    acc[...] += _mm(u, w2_ref[...], precise)

    @pl.when(j == pl.num_programs(1) - 1)
    def _():
        o_ref[...] = x_ref[...] + acc[...]


def _ffn(h, x2, w1, w3, w2):
    m = x2.shape[0]
    f = w1.shape[1]
    tm = _tile(m, 1280, 8)
    tf = _tile(f, 256, LANES)
    return pl.pallas_call(
        functools.partial(_ffn_kernel, precise=w1.dtype == F32),
        grid=(m // tm, f // tf),
        in_specs=[
            pl.BlockSpec((tm, D_MODEL), lambda i, j: (i, 0)),
            pl.BlockSpec((tm, D_MODEL), lambda i, j: (i, 0)),
            pl.BlockSpec((D_MODEL, tf), lambda i, j: (0, j)),
            pl.BlockSpec((D_MODEL, tf), lambda i, j: (0, j)),
            pl.BlockSpec((tf, D_MODEL), lambda i, j: (j, 0)),
        ],
        out_specs=pl.BlockSpec((tm, D_MODEL), lambda i, j: (i, 0)),
        out_shape=jax.ShapeDtypeStruct((m, D_MODEL), F32),
        scratch_shapes=[pltpu.VMEM((tm, D_MODEL), F32)],
        compiler_params=_params("parallel", "arbitrary"),
        name="ffn",
    )(h, x2, w1, w3, w2)


def _moe_kernel(te_ref, tv_ref, h_ref, w1_ref, w3_ref, w2_ref, o_ref, hb):
    del te_ref
    i = pl.program_id(0)
    j = pl.program_id(1)
    live = tv_ref[i] > 0

    @pl.when(j == 0)
    def _():
        o_ref[...] = jnp.zeros_like(o_ref)
        hb[...] = h_ref[...].astype(BF)

    @pl.when(live)
    def _():
        h = hb[...]
        u = jax.nn.silu(_dot(h, w1_ref[0])) * _dot(h, w3_ref[0])
        o_ref[...] += _dot(u.astype(BF), w2_ref[0])


def _moe_experts(tile_expert, tile_valid, hs, w1, w3, w2, tm):
    rows = hs.shape[0]
    f = w1.shape[2]
    tf = _tile(f, 512, LANES)
    nf = f // tf

    def fj(i, j, tv):
        return jnp.where(tv[i] > 0, j, nf - 1)

    grid_spec = pltpu.PrefetchScalarGridSpec(
        num_scalar_prefetch=2,
        grid=(rows // tm, nf),
        in_specs=[
            pl.BlockSpec((tm, D_MODEL), lambda i, j, te, tv: (i, 0)),
            pl.BlockSpec((1, D_MODEL, tf), lambda i, j, te, tv: (te[i], 0, fj(i, j, tv))),
            pl.BlockSpec((1, D_MODEL, tf), lambda i, j, te, tv: (te[i], 0, fj(i, j, tv))),
            pl.BlockSpec((1, tf, D_MODEL), lambda i, j, te, tv: (te[i], fj(i, j, tv), 0)),
        ],
        out_specs=pl.BlockSpec((tm, D_MODEL), lambda i, j, te, tv: (i, 0)),
        scratch_shapes=[pltpu.VMEM((tm, D_MODEL), BF)],
    )
    return pl.pallas_call(
        _moe_kernel,
        grid_spec=grid_spec,
        out_shape=jax.ShapeDtypeStruct((rows, D_MODEL), F32),
        compiler_params=_params("parallel", "arbitrary"),
        name="moe_experts",
    )(tile_expert, tile_valid, hs, w1, w3, w2)


def _moe_route(idx, tm):
    mt = idx.shape[0]
    e_flat = jnp.concatenate([idx[:, 0], idx[:, 1]])
    tok = jnp.concatenate([jnp.arange(mt, dtype=jnp.int32)] * 2)
    onehot = (e_flat[:, None] == jnp.arange(N_EXPERTS, dtype=jnp.int32)[None, :]).astype(jnp.int32)
    incl = jnp.cumsum(onehot, axis=0)
    rank = jnp.sum((incl - onehot) * onehot, axis=1)
    counts = incl[-1]
    tiles_per = (counts + tm - 1) // tm
    tile_end = jnp.cumsum(tiles_per)
    start = (tile_end - tiles_per) * tm
    dest = jnp.sum(onehot * start[None, :], axis=1) + rank
    n_tiles = (2 * mt) // tm + N_EXPERTS
    slot_tok = jnp.zeros((n_tiles * tm,), jnp.int32).at[dest].set(tok)
    ti = jnp.arange(n_tiles, dtype=jnp.int32)
    tile_valid = (ti < tile_end[-1]).astype(jnp.int32)
    te = jnp.sum((ti[:, None] >= tile_end[None, :]).astype(jnp.int32), axis=1)
    last_e = jnp.sum((tile_end[-1] - 1 >= tile_end).astype(jnp.int32))
    tile_expert = jnp.where(tile_valid > 0, te, last_e).astype(jnp.int32)
    return slot_tok, tile_expert, tile_valid, dest[:mt], dest[mt:]


def _combine_kernel(*refs, final):
    if final:
        x_ref, a_ref, b_ref, g_ref, gf_ref, o_ref = refs
    else:
        x_ref, a_ref, b_ref, g_ref, o_ref = refs
    g = g_ref[...]
    y = x_ref[...] + (g[:, 0:1] * a_ref[...] + g[:, 1:2] * b_ref[...])
    o_ref[...] = _rms(y, gf_ref[...]) if final else y


def _combine(x2, a, b, gate, gfin=None):
    m = x2.shape[0]
    tm = _tile(m, 640, 8)
    spec = pl.BlockSpec((tm, D_MODEL), lambda i: (i, 0))
    in_specs = [spec] * 3 + [pl.BlockSpec((tm, LANES), lambda i: (i, 0))]
    args = [x2, a, b, gate]
    if gfin is not None:
        in_specs.append(pl.BlockSpec((1, D_MODEL), lambda i: (0, 0)))
        args.append(gfin)
    return pl.pallas_call(
        functools.partial(_combine_kernel, final=gfin is not None),
        grid=(m // tm,), in_specs=in_specs, out_specs=spec,
        out_shape=jax.ShapeDtypeStruct((m, D_MODEL), F32),
        compiler_params=_params("parallel"), name="combine",
    )(*args)


def _final_kernel(x_ref, g_ref, o_ref):
    o_ref[...] = _rms(x_ref[...], g_ref[...])


def _final_norm(x2, g):
    m = x2.shape[0]
    tm = _tile(m, 640, 8)
    spec = pl.BlockSpec((tm, D_MODEL), lambda i: (i, 0))
    return pl.pallas_call(
        _final_kernel, grid=(m // tm,),
        in_specs=[spec, pl.BlockSpec((1, D_MODEL), lambda i: (0, 0))], out_specs=spec,
        out_shape=jax.ShapeDtypeStruct((m, D_MODEL), F32),
        compiler_params=_params("parallel"), name="final_norm",
    )(x2, g)


def _pack_w_in(w):
    offs = [0]
    for s in IN_SPLITS:
        offs.append(offs[-1] + s)
    seg = [w[:, offs[i]:offs[i + 1]] for i in range(len(IN_SPLITS))]
    qa, ka, va, ra, alr, qb, kb, vb, fb, ga, gb = seg
    wm = jnp.concatenate([va, ra, ga, gb, qa, ka, qb, kb, vb], axis=1)
    ws = jnp.concatenate([fb, alr, jnp.zeros((D_MODEL, LANES - FOX_HEADS - GLA_RANK), w.dtype)], axis=1)
    wkv_t = jnp.concatenate([kb, vb], axis=1).T.astype(BF)
    return wm, ws, wkv_t


def kernel(x_prompt, x_sample, cache_k, cache_v, cache_logf, state_gla, page_table, meta_tokens, norm_mix, w_in,
           w_alpha2, b_alpha, b_f, gla_norm, w_proj_a, w_proj_b, w_out, norm_ffn, w_ffn1, w_ffn3, w_ffn2, w_router,
           w_exp1, w_exp3, w_exp2, norm_final):
    b, seq, _ = x_prompt.shape
    db, n_dec, _ = x_sample.shape
    depth = w_in.shape[0]
    n_pool = cache_k.shape[1]
    n_valid = N_META + seq
    t = -(-n_valid // BLK) * BLK
    mp, msz = b * t, db * n_dec

    xp = jnp.concatenate([jnp.broadcast_to(meta_tokens[None].astype(F32), (b, N_META, D_MODEL)), x_prompt,
                          jnp.zeros((b, t - n_valid, D_MODEL), F32)], axis=1).reshape(mp, D_MODEL)
    xs = x_sample.reshape(msz, D_MODEL)
    cache_kt = jnp.transpose(cache_k, (0, 1, 3, 4, 2)).reshape(depth, n_pool, FOX_W, BLK)
    cache_vt = jnp.transpose(cache_v, (0, 1, 3, 4, 2)).reshape(depth, n_pool, FOX_W, BLK)
    page_rows = depth * n_pool * FOX_HEADS
    rows_per = _tile(page_rows, 2048, 8)
    lf_pages = jnp.transpose(cache_logf, (0, 1, 3, 2)).reshape(page_rows // rows_per, rows_per, BLK)
    cpage = _cumsum_lanes(lf_pages, running=False).reshape(depth, n_pool, FOX_HEADS, BLK)
    page_flat = page_table.reshape(-1).astype(jnp.int32)

    outs = {k: [] for k in ("kp", "vp", "lp", "sp", "ks", "vs", "ls", "ss")}
    y_prompt = y_sample = None
    for l in range(depth):
        wm32, ws32, wkv_t = _pack_w_in(w_in[l])
        wm, ws = wm32[:, :NZ_PROMPT].astype(BF), ws32.astype(BF)
        wst = ws.T
        wa32 = jnp.zeros((LANES, GLA_QK), F32).at[FOX_HEADS:FOX_HEADS + GLA_RANK].set(w_alpha2[l])
        wa = wa32.astype(BF)
        ba = b_alpha[l].reshape(1, GLA_QK)
        bfp = jnp.zeros((1, LANES), F32).at[0, :FOX_HEADS].set(b_f[l])
        bfc = b_f[l].reshape(FOX_HEADS, 1)
        g_mix = norm_mix[l].reshape(1, D_MODEL)
        gn = gla_norm[l].reshape(1, GLA_DV)
        wpa, wpb, wo = w_proj_a[l].astype(BF), w_proj_b[l].astype(BF), w_out[l].astype(BF)
        nf = norm_ffn[l].reshape(1, D_MODEL)
        is_moe = l % 2 == 1
        last = l == depth - 1
        jx = l // 2

        z_p, la_p, kt_p, vt_p, lft_p = _inproj(xp, g_mix, wm, ws, wa, ba, bfp,
                                               head_major=(b, wkv_t, wst, bfc))
        z3 = z_p.reshape(b, t, NZ_PROMPT)
        c_rows = _cumsum_lanes(lft_p, running=True)
        ob_p = _fox_prompt(z3, kt_p, vt_p, c_rows.reshape(b, FOX_HEADS // 2, 2, t))
        og_p, s_p = _gla_prompt(z3, la_p.reshape(b, t, GLA_QK), n_valid)
        z_s, la_s, lf_s = _inproj(xs, g_mix, wm32, ws32, wa32, ba, bfp)
        zs3 = z_s.reshape(db, n_dec, NZ)
        og_s, s_s = _gla_sample(zs3, la_s.reshape(db, n_dec, GLA_QK), state_gla[l])
        kn3 = zs3[:, :, KB:KB + FOX_W]
        vn3 = zs3[:, :, VB:VB + FOX_W]
        ob_s = _fox_sample(page_flat, cache_kt, cache_vt, cpage, l, zs3, kn3, vn3,
                           lf_s.reshape(db, n_dec, LANES))

        outs["kp"].append(kt_p[..., :n_valid])
        outs["vp"].append(vt_p[..., :n_valid])
        outs["lp"].append(lft_p[..., :n_valid])
        outs["sp"].append(s_p)
        outs["ks"].append(kn3.reshape(db, n_dec, FOX_HEADS, FOX_HD))
        outs["vs"].append(vn3.reshape(db, n_dec, FOX_HEADS, FOX_HD))
        outs["ls"].append(lf_s.reshape(db, n_dec, LANES)[:, :, :FOX_HEADS])
        outs["ss"].append(s_s)

        og_p2, ob_p2 = og_p.reshape(mp, GLA_V), ob_p.reshape(mp, FOX_W)
        og_s2, ob_s2 = og_s.reshape(msz, GLA_V), ob_s.reshape(msz, FOX_W)
        if not is_moe:
            xp, h_p = _merge(xp, z_p, og_p2, ob_p2, gn, wpa, wpb, wo, nf)
            xs, h_s = _merge(xs, z_s, og_s2, ob_s2, gn, w_proj_a[l], w_proj_b[l], w_out[l], nf)
            xs = _ffn(h_s, xs, w_ffn1[jx], w_ffn3[jx], w_ffn2[jx])
            w1, w3, w2 = w_ffn1[jx].astype(BF), w_ffn3[jx].astype(BF), w_ffn2[jx].astype(BF)
            xp = _ffn(h_p, xp, w1, w3, w2)
        else:
            wr = jnp.zeros((D_MODEL, LANES), F32).at[:, :N_EXPERTS].set(w_router[jx])
            wr1 = wr.astype(BF)
            wr = jnp.stack([wr1, (wr - wr1.astype(F32)).astype(BF)])
            xp, h_p, idx_p, gate_p = _merge(xp, z_p, og_p2, ob_p2, gn, wpa, wpb, wo, nf, wr)
            xs, h_s, idx_s, gate_s = _merge(xs, z_s, og_s2, ob_s2, gn, w_proj_a[l], w_proj_b[l], w_out[l], nf, wr)
            h_all = jnp.concatenate([h_p, h_s], axis=0)
            idx = jnp.concatenate([idx_p[:, :2], idx_s[:, :2]], axis=0)
            tm = 1024 if 2 * (mp + msz) >= 8 * 1024 else 256
            slot_tok, tile_e, tile_v, d1, d2 = _moe_route(idx, tm)
            hs = jnp.take(h_all, slot_tok, axis=0)
            ys = _moe_experts(tile_e, tile_v, hs, w_exp1[jx].astype(BF), w_exp3[jx].astype(BF),
                              w_exp2[jx].astype(BF), tm)
            gfin = norm_final.reshape(1, D_MODEL) if last else None
            yp = _combine(xp, jnp.take(ys, d1[:mp], axis=0), jnp.take(ys, d2[:mp], axis=0), gate_p, gfin)
            ysm = _combine(xs, jnp.take(ys, d1[mp:], axis=0), jnp.take(ys, d2[mp:], axis=0), gate_s, gfin)
            if last:
                y_prompt, y_sample = yp, ysm
            else:
                xp, xs = yp, ysm

    if y_prompt is None:
        g_fin = norm_final.reshape(1, D_MODEL)
        y_prompt, y_sample = _final_norm(xp, g_fin), _final_norm(xs, g_fin)
    y_prompt = y_prompt.reshape(b, t, D_MODEL)[:, N_META:n_valid]
    y_sample = y_sample.reshape(db, n_dec, D_MODEL)
    st = jnp.stack
    return (y_prompt, y_sample,
            jnp.transpose(st(outs["kp"]), (0, 1, 4, 2, 3)), jnp.transpose(st(outs["vp"]), (0, 1, 4, 2, 3)),
            jnp.transpose(st(outs["lp"]), (0, 1, 3, 2)), st(outs["sp"]),
            st(outs["ks"]), st(outs["vs"]), st(outs["ls"]), st(outs["ss"]))
```

```python
import functools

import jax
import jax.numpy as jnp
from jax import lax
from jax.experimental import pallas as pl
from jax.experimental.pallas import tpu as pltpu

BF = jnp.bfloat16
F32 = jnp.float32

D_MODEL = 1024
N_META = 16
BLK = 128
GLA_HEADS = 4
GLA_DK = 128
GLA_DV = 256
GLA_RANK = 16
GLA_TAU = 16.0
GLA_SUB = 16
FOX_HEADS = 8
FOX_HD = 64
N_EXPERTS = 8
RMS_EPS = 1e-6
NEG = -1e30
LANES = 128

GLA_QK = GLA_HEADS * GLA_DK
GLA_V = GLA_HEADS * GLA_DV
FOX_W = FOX_HEADS * FOX_HD
VA, RA, GA, GB = 0, 1024, 2048, 3072
QA, KA = 4096, 4608
QB, KB, VB = 5120, 5632, 6144
NZ_PROMPT = 5632
NZ = 6656
IN_SPLITS = (GLA_QK, GLA_QK, GLA_V, GLA_V, GLA_RANK, FOX_W, FOX_W, FOX_W, FOX_HEADS, D_MODEL, D_MODEL)

VMEM_LIMIT = 56 * 1024 * 1024


def _params(*sem):
    return pltpu.CompilerParams(dimension_semantics=sem, vmem_limit_bytes=VMEM_LIMIT)


def _tile(n, target, mult):
    best = None
    for t in range(mult, min(n, target) + 1, mult):
        if n % t == 0:
            best = t
    return best if best is not None else n


def _dot(a, b):
    return jnp.dot(a, b, preferred_element_type=F32)


def _dot_nt(a, b):
    return lax.dot_general(a, b, (((1,), (1,)), ((), ())), preferred_element_type=F32)


def _split2(x):
    hi = x.astype(BF)
    return hi, (x - hi.astype(F32)).astype(BF)


def _dot3(a, b, nt=False):
    dot = _dot_nt if nt else _dot
    ah, al = _split2(a)
    bh, bl = _split2(b)
    return (dot(al, bh) + dot(ah, bl)) + dot(ah, bh)


def _mm(a, w, precise):
    return _dot3(a, w) if precise else _dot(a.astype(BF), w.astype(BF))


def _split3(x):
    h1 = x.astype(BF)
    r1 = x - h1.astype(F32)
    h2 = r1.astype(BF)
    r2 = r1 - h2.astype(F32)
    return h1, h2, r2.astype(BF)


def _dot_exact_r(x, m):
    h1, h2, h3 = _split3(x)
    return (_dot(h3, m) + _dot(h2, m)) + _dot(h1, m)


def _dot_exact_l(m, x):
    h1, h2, h3 = _split3(x)
    return (_dot(m, h3) + _dot(m, h2)) + _dot(m, h1)


def _logsig(x):
    return jnp.minimum(x, 0.0) - jnp.log1p(jnp.exp(-jnp.abs(x)))


def _rms(x, g):
    ms = jnp.mean(x * x, axis=-1, keepdims=True)
    return x * lax.rsqrt(ms + RMS_EPS) * g


def _inproj_kernel(*refs, head_major):
    if head_major:
        (x_ref, g_ref, wm_ref, ws_ref, wa_ref, ba_ref, bf_ref, wkv_ref, wst_ref, bfc_ref,
         z_ref, la_ref, kt_ref, vt_ref, lft_ref, h_scr) = refs
    else:
        x_ref, g_ref, wm_ref, ws_ref, wa_ref, ba_ref, bf_ref, z_ref, la_ref, lf_ref, h_scr = refs

    precise = not head_major

    @pl.when(pl.program_id(1) == 0)
    def _():
        hf = _rms(x_ref[...], g_ref[...])
        h, h_lo = _split2(hf)
        h_scr[0] = h
        h_scr[1] = h_lo
        s = _mm(hf, ws_ref[...], precise)
        a = _mm(s, wa_ref[...], precise) + ba_ref[...]
        la_ref[...] = _logsig(a) * (1.0 / GLA_TAU)
        if head_major:
            tm = h.shape[0]
            kv = _dot_nt(wkv_ref[...], h)
            kt_ref[0] = kv[:FOX_W].reshape(FOX_HEADS, FOX_HD, tm)
            vt_ref[0] = kv[FOX_W:].reshape(FOX_HEADS, FOX_HD, tm)
            st = _dot_nt(wst_ref[...], h)
            lft_ref[0] = _logsig(st[:FOX_HEADS] + bfc_ref[...])
        else:
            lf_ref[...] = _logsig(s + bf_ref[...])

    if precise:
        wh, wl = _split2(wm_ref[...])
        z_ref[...] = (_dot(h_scr[1], wh) + _dot(h_scr[0], wl)) + _dot(h_scr[0], wh)
    else:
        z_ref[...] = _dot(h_scr[0], wm_ref[...])


def _inproj(x2, g, wm, ws, wa, ba, bfp, head_major=None):
    m = x2.shape[0]
    nz = wm.shape[1]
    const = lambda shape: pl.BlockSpec(shape, lambda i, j: (0,) * len(shape))
    if head_major is not None:
        b, wkv, wst, bfc = head_major
        t = m // b
        tm = _tile(t, 640, BLK)
        nt = t // tm
    else:
        tm = _tile(m, 1280, 8)
    tn = _tile(nz, 1664, LANES)
    in_specs = [
        pl.BlockSpec((tm, D_MODEL), lambda i, j: (i, 0)),
        const((1, D_MODEL)),
        pl.BlockSpec((D_MODEL, tn), lambda i, j: (0, j)),
        const((D_MODEL, LANES)), const((LANES, GLA_QK)), const((1, GLA_QK)), const((1, LANES)),
    ]
    args = [x2, g, wm, ws, wa, ba, bfp]
    out_specs = [pl.BlockSpec((tm, tn), lambda i, j: (i, j)),
                 pl.BlockSpec((tm, GLA_QK), lambda i, j: (i, 0))]
    out_shape = [jax.ShapeDtypeStruct((m, nz), F32), jax.ShapeDtypeStruct((m, GLA_QK), F32)]
    if head_major is not None:
        in_specs += [const((2 * FOX_W, D_MODEL)), const((LANES, D_MODEL)), const((FOX_HEADS, 1))]
        args += [wkv, wst, bfc]
        hm = pl.BlockSpec((1, FOX_HEADS, FOX_HD, tm), lambda i, j: (i // nt, 0, 0, i % nt))
        out_specs += [hm, hm, pl.BlockSpec((1, FOX_HEADS, tm), lambda i, j: (i // nt, 0, i % nt))]
        out_shape += [jax.ShapeDtypeStruct((b, FOX_HEADS, FOX_HD, t), F32)] * 2
        out_shape += [jax.ShapeDtypeStruct((b, FOX_HEADS, t), F32)]
    else:
        out_specs.append(pl.BlockSpec((tm, LANES), lambda i, j: (i, 0)))
        out_shape.append(jax.ShapeDtypeStruct((m, LANES), F32))
    return pl.pallas_call(
        functools.partial(_inproj_kernel, head_major=head_major is not None),
        grid=(m // tm, nz // tn),
        in_specs=in_specs, out_specs=out_specs, out_shape=out_shape,
        scratch_shapes=[pltpu.VMEM((2, tm, D_MODEL), BF)],
        compiler_params=_params("parallel", "arbitrary"),
        name="inproj",
    )(*args)


def _cumsum_kernel(x_ref, c_ref, carry, *, nsub, running):
    if running:
        @pl.when(pl.program_id(1) == 0)
        def _():
            carry[...] = jnp.zeros_like(carry)

    r = lax.broadcasted_iota(jnp.int32, (BLK, BLK), 0)
    c = lax.broadcasted_iota(jnp.int32, (BLK, BLK), 1)
    upper = (r <= c).astype(BF)
    for u in range(nsub):
        cb = _dot_exact_r(x_ref[0, :, u * BLK:(u + 1) * BLK], upper)
        if running:
            cb = cb + carry[...]
            carry[...] = cb[:, BLK - 1:BLK]
        c_ref[0, :, u * BLK:(u + 1) * BLK] = cb


def _cumsum_lanes(x3, running):
    a, rows, t = x3.shape
    tb = _tile(t, 640, BLK)
    return pl.pallas_call(
        functools.partial(_cumsum_kernel, nsub=tb // BLK, running=running),
        grid=(a, t // tb),
        in_specs=[pl.BlockSpec((1, rows, tb), lambda i, j: (i, 0, j))],
        out_specs=pl.BlockSpec((1, rows, tb), lambda i, j: (i, 0, j)),
        out_shape=jax.ShapeDtypeStruct(x3.shape, F32),
        scratch_shapes=[pltpu.VMEM((rows, 1), F32)],
        compiler_params=_params("parallel", "arbitrary"),
        name="cumsum_lanes",
    )(x3)


def _fox_prompt_kernel(q_ref, k_ref, v_ref, ck_ref, o_ref, ka, va, m_scr, acc_scr, *, tq):
    qi = pl.program_id(2)
    t = ka.shape[2]

    @pl.when(qi == 0)
    def _():
        row = lax.broadcasted_iota(jnp.int32, (FOX_HD, t), 0)
        for h in range(2):
            c1, c2, c3 = _split3(-ck_ref[0, 0, h:h + 1, :])
            extra = jnp.where(row == 0, c1.astype(F32), jnp.where(row == 1, c2.astype(F32),
                              jnp.where(row == 2, c3.astype(F32), 0.0)))
            ka[h] = jnp.concatenate([k_ref[0, h], extra], axis=0).astype(BF)
            va[h] = jnp.concatenate([v_ref[0, h], jnp.where(row == 0, 1.0, 0.0)], axis=0).astype(BF)

    lane = lax.broadcasted_iota(jnp.int32, (1, LANES), 1)
    ones3 = jnp.where((lane >= FOX_HD) & (lane < FOX_HD + 3), 1.0, 0.0)
    q2 = q_ref[0] * (FOX_HD ** -0.5)
    qh = (jnp.where(lane < FOX_HD, q2, ones3).astype(BF),
          jnp.where(lane < FOX_HD, pltpu.roll(q2, FOX_HD, 1), ones3).astype(BF))
    m_scr[...] = jnp.full(m_scr.shape, NEG, F32)
    acc_scr[...] = jnp.zeros(acc_scr.shape, F32)

    def logits(j):
        start = pl.multiple_of(j * tq, tq)
        return tuple(_dot(qh[h], ka[h, :, pl.ds(start, tq)]) for h in range(2))

    def attend(j, ss, causal):
        start = pl.multiple_of(j * tq, tq)
        for h in range(2):
            s = ss[h]
            if causal:
                ri = lax.broadcasted_iota(jnp.int32, (tq, tq), 0)
                ci = lax.broadcasted_iota(jnp.int32, (tq, tq), 1)
                s = jnp.where(ci <= ri, s, NEG)
            m_old = m_scr[h]
            m_new = jnp.maximum(m_old, jnp.max(s, axis=-1, keepdims=True))
            p = jnp.exp(s - m_new).astype(BF)
            acc_scr[h] = jnp.exp(m_old - m_new) * acc_scr[h] + _dot_nt(p, va[h, :, pl.ds(start, tq)])
            m_scr[h] = m_new

    def tiles(*js_causal):
        ss = [logits(j) for j, _ in js_causal]
        for (j, causal), s in zip(js_causal, ss):
            attend(j, s, causal)

    def body(j2, carry):
        tiles((2 * j2, False), (2 * j2 + 1, False))
        return carry

    lax.fori_loop(0, qi // 2, body, 0)

    @pl.when((qi & 1) == 1)
    def _():
        tiles((qi - 1, False), (qi, True))

    @pl.when((qi & 1) == 0)
    def _():
        tiles((qi, True))
    o0 = acc_scr[0]
    o1 = acc_scr[1]
    o0 = o0 / o0[:, FOX_HD:FOX_HD + 1]
    o1 = o1 / o1[:, FOX_HD:FOX_HD + 1]
    o_ref[0] = jnp.where(lane < FOX_HD, o0, pltpu.roll(o1, FOX_HD, 1))


def _fox_prompt(z3, kt, vt, ck4):
    b, t, _ = z3.shape
    tq = _tile(t, 640, BLK)
    npair = FOX_HEADS // 2
    kv_spec = pl.BlockSpec((1, 2, FOX_HD, t), lambda i, hp, qi: (i, hp, 0, 0))
    return pl.pallas_call(
        functools.partial(_fox_prompt_kernel, tq=tq),
        grid=(b, npair, t // tq),
        in_specs=[
            pl.BlockSpec((1, tq, LANES), lambda i, hp, qi: (i, qi, QB // LANES + hp)),
            kv_spec, kv_spec,
            pl.BlockSpec((1, 1, 2, t), lambda i, hp, qi: (i, hp, 0, 0)),
        ],
        out_specs=pl.BlockSpec((1, tq, LANES), lambda i, hp, qi: (i, qi, hp)),
        out_shape=jax.ShapeDtypeStruct((b, t, FOX_W), F32),
        scratch_shapes=[
            pltpu.VMEM((2, 2 * FOX_HD, t), BF), pltpu.VMEM((2, 2 * FOX_HD, t), BF),
            pltpu.VMEM((2, tq, 1), F32), pltpu.VMEM((2, tq, LANES), F32),
        ],
        compiler_params=_params("parallel", "parallel", "arbitrary"),
        name="fox_prompt",
    )(z3, kt, vt, ck4)


def _gla_masks():
    r = jnp.arange(BLK)[:, None]
    c = jnp.arange(BLK)[None, :]
    masks = [(c < GLA_SUB) & ((r % GLA_SUB) + c < GLA_SUB)]
    m = BLK // 2
    while m >= GLA_SUB:
        masks.append((r // (2 * m) == c // (2 * m)) & (r % (2 * m) >= m) & (c % (2 * m) < m))
        m //= 2
    k = jnp.arange(GLA_SUB * GLA_DK)[:, None]
    collect = (k // GLA_DK == c).astype(BF)
    return jnp.stack(masks).astype(F32), collect


def _gla_chunk_math(qs, ks, vs, las, s0s, msk_ref, col_ref, pq, pc, prod):
    heads = range(len(qs))
    ri = lax.broadcasted_iota(jnp.int32, (BLK, BLK), 0)
    ci = lax.broadcasted_iota(jnp.int32, (BLK, BLK), 1)
    lower = (ri >= ci).astype(BF)
    cums = [_dot_exact_l(lower, las[h]) for h in heads]
    for h in heads:
        pq[h, 0:BLK, :] = qs[h]
        pc[h, 0:BLK, :] = cums[h]
        pq[h, BLK:, :] = jnp.zeros((GLA_SUB, GLA_DK), F32)
        pc[h, BLK:, :] = jnp.zeros((GLA_SUB, GLA_DK), F32)
    for h in heads:
        for d in range(GLA_SUB):
            e = jnp.exp(jnp.minimum(pc[h, d:d + BLK, :] - cums[h], 0.0))
            prod[h, :, d * GLA_DK:(d + 1) * GLA_DK] = (pq[h, d:d + BLK, :] * ks[h] * e).astype(BF)
    bands = [_dot(prod[h], col_ref[...]) for h in heads]
    a = [pltpu.roll(jnp.where(msk_ref[0] > 0.5, bands[h], 0.0), 0, 1, stride=1, stride_axis=0).T
         for h in heads]
    m = BLK // 2
    lvl = 1
    while m >= GLA_SUB:
        nb = BLK // (2 * m)
        up = (ri & (2 * m - 1)) >= m
        for h in heads:
            cum = cums[h]
            pieces = [jnp.broadcast_to(cum[2 * m * u + m - 1:2 * m * u + m, :], (2 * m, GLA_DK)) for u in range(nb)]
            bnd = pieces[0] if nb == 1 else jnp.concatenate(pieces, axis=0)
            dq = cum - bnd
            qt = jnp.where(up, qs[h] * jnp.exp(jnp.minimum(dq, 0.0)), 0.0).astype(BF)
            kt = jnp.where(up, 0.0, ks[h] * jnp.exp(jnp.minimum(-dq, 0.0))).astype(BF)
            a[h] = jnp.where(msk_ref[lvl] > 0.5, _dot_nt(qt, kt), a[h])
        m //= 2
        lvl += 1
    outs = []
    for h in heads:
        cum = cums[h]
        vb = vs[h].astype(BF)
        o = _dot(a[h].astype(BF), vb) + _dot((qs[h] * jnp.exp(cum)).astype(BF), s0s[h].astype(BF))
        last = cum[BLK - 1:BLK, :]
        kd_t = (ks[h] * jnp.exp(last - cum)).T.astype(BF)
        last_col = jnp.sum(jnp.where(ri == ci, last, 0.0), axis=1, keepdims=True)
        outs.append((o, jnp.exp(last_col) * s0s[h] + _dot(kd_t, vb)))
    return outs


def _gla_prompt_kernel(q_ref, k_ref, v_ref, la_ref, msk_ref, col_ref, o_ref, s_ref, s_scr, pq, pc, prod, *, n_valid):
    c = pl.program_id(1)

    @pl.when(c == 0)
    def _():
        s_scr[...] = jnp.zeros_like(s_scr)

    real = (c * BLK + lax.broadcasted_iota(jnp.int32, (BLK, GLA_DK), 0)) < n_valid
    dk = lambda h: slice(h * GLA_DK, (h + 1) * GLA_DK)
    dv = lambda h: slice(h * GLA_DV, (h + 1) * GLA_DV)
    heads = range(GLA_HEADS)
    qs = [q_ref[0, :, dk(h)] * (GLA_DK ** -0.5) for h in heads]
    ks = [jnp.where(real, k_ref[0, :, dk(h)], 0.0) for h in heads]
    las = [jnp.where(real, la_ref[0, :, dk(h)], 0.0) for h in heads]
    vs = [v_ref[0, :, dv(h)] for h in heads]
    outs = _gla_chunk_math(qs, ks, vs, las, [s_scr[h] for h in heads], msk_ref, col_ref, pq, pc, prod)
    for h in heads:
        o, s1 = outs[h]
        o_ref[0, :, dv(h)] = o
        s_scr[h] = s1
        s_ref[0, h] = s1


def _gla_prompt(z3, la3, n_valid):
    b, t, _ = z3.shape
    masks, collect = _gla_masks()
    return pl.pallas_call(
        functools.partial(_gla_prompt_kernel, n_valid=n_valid),
        grid=(b, t // BLK),
        in_specs=[
            pl.BlockSpec((1, BLK, GLA_QK), lambda i, c: (i, c, QA // GLA_QK)),
            pl.BlockSpec((1, BLK, GLA_QK), lambda i, c: (i, c, KA // GLA_QK)),
            pl.BlockSpec((1, BLK, GLA_V), lambda i, c: (i, c, VA // GLA_V)),
            pl.BlockSpec((1, BLK, GLA_QK), lambda i, c: (i, c, 0)),
            pl.BlockSpec(masks.shape, lambda i, c: (0, 0, 0)),
            pl.BlockSpec(collect.shape, lambda i, c: (0, 0)),
        ],
        out_specs=[
            pl.BlockSpec((1, BLK, GLA_V), lambda i, c: (i, c, 0)),
            pl.BlockSpec((1, GLA_HEADS, GLA_DK, GLA_DV), lambda i, c: (i, 0, 0, 0)),
        ],
        out_shape=[
            jax.ShapeDtypeStruct((b, t, GLA_V), F32),
            jax.ShapeDtypeStruct((b, GLA_HEADS, GLA_DK, GLA_DV), F32),
        ],
        scratch_shapes=[pltpu.VMEM((GLA_HEADS, GLA_DK, GLA_DV), F32),
                        pltpu.VMEM((GLA_HEADS, BLK + GLA_SUB, GLA_DK), F32),
                        pltpu.VMEM((GLA_HEADS, BLK + GLA_SUB, GLA_DK), F32),
                        pltpu.VMEM((GLA_HEADS, BLK, GLA_SUB * GLA_DK), BF)],
        compiler_params=_params("parallel", "arbitrary"),
        name="gla_prompt",
    )(z3, z3, z3, la3, masks, collect)


def _gla_sample_kernel(q_ref, k_ref, v_ref, la_ref, s0_ref, o_ref, s1_ref, pq, pk, pc, pv, *, n):
    ri = lax.broadcasted_iota(jnp.int32, (n, GLA_DK), 0)
    for h in range(GLA_HEADS):
        q = q_ref[0, :, h * GLA_DK:(h + 1) * GLA_DK] * (GLA_DK ** -0.5)
        k = k_ref[0, :, h * GLA_DK:(h + 1) * GLA_DK]
        la = la_ref[0, :, h * GLA_DK:(h + 1) * GLA_DK]
        v = v_ref[0, :, h * GLA_DV:(h + 1) * GLA_DV]
        s0 = s0_ref[0, h]
        cum = jnp.zeros_like(la)
        for j in range(n):
            cum = cum + jnp.where(ri >= j, la[j:j + 1, :], 0.0)
        o = jnp.zeros((n, GLA_DV), F32)
        for s in range(n):
            w = q * jnp.exp(jnp.minimum(cum - cum[s:s + 1, :], 0.0)) * k[s:s + 1, :]
            sc = jnp.sum(w, axis=-1, keepdims=True)
            o = o + jnp.where(ri[:, :1] >= s, sc, 0.0) * v[s:s + 1, :]
        for scr, val in ((pq, q * jnp.exp(cum)), (pk, k), (pc, cum)):
            scr[...] = jnp.zeros_like(scr)
            scr[0:n, :] = val
        pv[...] = jnp.zeros_like(pv)
        pv[0:n, :] = v
        o_inter = _dot3(pq[...], s0)
        o_ref[0, :, h * GLA_DV:(h + 1) * GLA_DV] = o + o_inter[0:n, :]
        cum_t = pc[...].T
        k_t = pk[...].T
        last = cum_t[:, n - 1:n]
        kd = k_t * jnp.exp(jnp.minimum(last - cum_t, 0.0))
        s1_ref[0, h] = jnp.exp(last) * s0 + _dot3(kd, pv[...])


def _gla_sample(z3, la3, state):
    db, n, _ = z3.shape
    return pl.pallas_call(
        functools.partial(_gla_sample_kernel, n=n),
        grid=(db,),
        in_specs=[
            pl.BlockSpec((1, n, GLA_QK), lambda i: (i, 0, QA // GLA_QK)),
            pl.BlockSpec((1, n, GLA_QK), lambda i: (i, 0, KA // GLA_QK)),
            pl.BlockSpec((1, n, GLA_V), lambda i: (i, 0, VA // GLA_V)),
            pl.BlockSpec((1, n, GLA_QK), lambda i: (i, 0, 0)),
            pl.BlockSpec((1, GLA_HEADS, GLA_DK, GLA_DV), lambda i: (i, 0, 0, 0)),
        ],
        out_specs=[
            pl.BlockSpec((1, n, GLA_V), lambda i: (i, 0, 0)),
            pl.BlockSpec((1, GLA_HEADS, GLA_DK, GLA_DV), lambda i: (i, 0, 0, 0)),
        ],
        out_shape=[
            jax.ShapeDtypeStruct((db, n, GLA_V), F32),
            jax.ShapeDtypeStruct(state.shape, F32),
        ],
        scratch_shapes=[pltpu.VMEM((BLK, GLA_DK), F32), pltpu.VMEM((BLK, GLA_DK), F32),
                        pltpu.VMEM((BLK, GLA_DK), F32), pltpu.VMEM((BLK, GLA_DV), F32)],
        compiler_params=_params("parallel"),
        name="gla_sample",
    )(z3, z3, z3, la3, state)


def _fox_sample_kernel(pt_ref, *refs, pps, n):
    del pt_ref
    k_refs = refs[0:pps]
    v_refs = refs[pps:2 * pps]
    c_refs = refs[2 * pps:3 * pps]
    q_ref, kn_ref, vn_ref, lfn_ref, o_ref, qbd, carry, m_scr, l_scr, acc_scr = refs[3 * pps:]
    j = pl.program_id(1)
    rows = n * FOX_HEADS
    rr = lax.broadcasted_iota(jnp.int32, (rows, FOX_W), 0)
    cc = lax.broadcasted_iota(jnp.int32, (rows, FOX_W), 1)
    own = (cc >> 6) == (rr & (FOX_HEADS - 1))

    @pl.when(j == 0)
    def _():
        qrep = jnp.concatenate([jnp.broadcast_to(q_ref[0, t:t + 1, :], (FOX_HEADS, FOX_W)) for t in range(n)], axis=0)
        qbd[...] = jnp.where(own, qrep * (FOX_HD ** -0.5), 0.0)
        carry[...] = jnp.zeros_like(carry)
        m_scr[...] = jnp.full(m_scr.shape, NEG, F32)
        l_scr[...] = jnp.zeros_like(l_scr)
        acc_scr[...] = jnp.zeros_like(acc_scr)

    qh, ql = _split2(qbd[...])
    qq = jnp.concatenate([qh, ql], axis=0)
    run = carry[...]
    ss = []
    for r in range(pps):
        cg = run + c_refs[r][0, 0]
        run = cg[:, BLK - 1:BLK]
        kh, kl = _split2(k_refs[r][0, 0])
        t = _dot(qq, kh)
        ss.append(((t[rows:] + _dot(qh, kl)) + t[:rows]) - jnp.concatenate([cg] * n, axis=0))
    carry[...] = run
    s = jnp.concatenate(ss, axis=1)
    m_old = m_scr[...]
    m_new = jnp.maximum(m_old, jnp.max(s, axis=-1, keepdims=True))
    p = jnp.exp(s - m_new)
    alpha = jnp.exp(m_old - m_new)
    l_scr[...] = alpha * l_scr[...] + jnp.sum(p, axis=-1, keepdims=True)
    ph, pl_ = _split2(p)
    pv = None
    for r in range(pps):
        cols = slice(r * BLK, (r + 1) * BLK)
        vh, vl = _split2(v_refs[r][0, 0])
        t = _dot_nt(jnp.concatenate([ph[:, cols], pl_[:, cols]], axis=0), vh)
        t3 = (t[rows:] + _dot_nt(ph[:, cols], vl)) + t[:rows]
        pv = t3 if pv is None else pv + t3
    acc_scr[...] = alpha * acc_scr[...] + pv
    m_scr[...] = m_new

    @pl.when(j == pl.num_programs(1) - 1)
    def _():
        lfn = lfn_ref[0]
        ri = lax.broadcasted_iota(jnp.int32, (n, LANES), 0)
        cn = jnp.zeros_like(lfn)
        for i in range(n):
            cn = cn + jnp.where(ri >= i, lfn[i:i + 1, :], 0.0)
        er = lax.broadcasted_iota(jnp.int32, (rows, LANES), 0)
        ec = lax.broadcasted_iota(jnp.int32, (rows, LANES), 1)
        sel = (er & (FOX_HEADS - 1)) == ec
        tot = jnp.concatenate([carry[...]] * n, axis=0)
        qf = qbd[...]
        tok = rr[:, :1] >> 3
        m_run, l_run, acc = m_scr[...], l_scr[...], acc_scr[...]
        for i in range(n):
            ci = jnp.sum(jnp.where(sel, cn[i:i + 1, :], 0.0), axis=-1, keepdims=True)
            s = jnp.sum(qf * kn_ref[0, i:i + 1, :], axis=-1, keepdims=True) - (tot + ci)
            s = jnp.where(tok >= i, s, NEG)
            m_new = jnp.maximum(m_run, s)
            p = jnp.exp(s - m_new)
            alpha = jnp.exp(m_run - m_new)
            l_run = alpha * l_run + p
            acc = alpha * acc + p * vn_ref[0, i:i + 1, :]
            m_run = m_new
        o_full = jnp.where(own, acc / l_run, 0.0)
        o_ref[0] = jnp.sum(o_full.reshape(n, FOX_HEADS, FOX_W), axis=1)


def _fox_sample(page_flat, cache_kt, cache_vt, cpage, layer, q3, kn3, vn3, lfn3):
    db, n, _ = q3.shape
    n_pages = page_flat.shape[0] // db
    pps = _tile(n_pages, 8, 1)
    rows = n * FOX_HEADS

    def page_map(r):
        return lambda i, j, pt: (layer, pt[i * n_pages + j * pps + r], 0, 0)

    def kv_specs():
        return [pl.BlockSpec((1, 1, FOX_W, BLK), page_map(r)) for r in range(pps)]

    c_specs = [pl.BlockSpec((1, 1, FOX_HEADS, BLK), page_map(r)) for r in range(pps)]
    grid_spec = pltpu.PrefetchScalarGridSpec(
        num_scalar_prefetch=1,
        grid=(db, n_pages // pps),
        in_specs=kv_specs() + kv_specs() + c_specs + [
            pl.BlockSpec((1, n, FOX_W), lambda i, j, pt: (i, 0, QB // FOX_W)),
            pl.BlockSpec((1, n, FOX_W), lambda i, j, pt: (i, 0, 0)),
            pl.BlockSpec((1, n, FOX_W), lambda i, j, pt: (i, 0, 0)),
            pl.BlockSpec((1, n, LANES), lambda i, j, pt: (i, 0, 0)),
        ],
        out_specs=pl.BlockSpec((1, n, FOX_W), lambda i, j, pt: (i, 0, 0)),
        scratch_shapes=[
            pltpu.VMEM((rows, FOX_W), F32), pltpu.VMEM((FOX_HEADS, 1), F32),
            pltpu.VMEM((rows, 1), F32), pltpu.VMEM((rows, 1), F32), pltpu.VMEM((rows, FOX_W), F32),
        ],
    )
    return pl.pallas_call(
        functools.partial(_fox_sample_kernel, pps=pps, n=n),
        grid_spec=grid_spec,
        out_shape=jax.ShapeDtypeStruct((db, n, FOX_W), F32),
        compiler_params=_params("parallel", "arbitrary"),
        name="fox_sample",
    )(page_flat, *([cache_kt] * pps), *([cache_vt] * pps), *([cpage] * pps), q3, kn3, vn3, lfn3)


def _merge_kernel(*refs, moe, precise):
    (x_ref, og_ref, r_ref, ob_ref, ga_ref, gb_ref, gn_ref, wpa_ref, wpb_ref, wo_ref, nf_ref) = refs[:11]
    if moe:
        wr_ref, xo_ref, h_ref, idx_ref, gate_ref = refs[11:]
    else:
        xo_ref, h_ref = refs[11:]
    og = og_ref[...]
    gn = gn_ref[...]
    oa = jnp.concatenate([_rms(og[:, h * GLA_DV:(h + 1) * GLA_DV], gn) for h in range(GLA_HEADS)], axis=-1)
    oa = oa * jax.nn.silu(r_ref[...])
    pa = _mm(oa, wpa_ref[...], precise)
    pb = _mm(ob_ref[...], wpb_ref[...], precise)
    y = jax.nn.sigmoid(ga_ref[...]) * pa + jax.nn.sigmoid(gb_ref[...]) * pb
    xn = x_ref[...] + _mm(y, wo_ref[...], precise)
    xo_ref[...] = xn
    h = _rms(xn, nf_ref[...])
    if moe:
        h_ref[...] = h
        h1 = h.astype(BF)
        h2 = (h - h1.astype(F32)).astype(BF)
        w1 = wr_ref[0]
        w2 = wr_ref[1]
        logits = (_dot(h2, w1) + _dot(h1, w2)) + _dot(h1, w1)
        lane = lax.broadcasted_iota(jnp.int32, logits.shape, 1)
        logits = jnp.where(lane < N_EXPERTS, logits, -jnp.inf)
        v1 = jnp.max(logits, axis=-1, keepdims=True)
        i1 = jnp.min(jnp.where(logits == v1, lane, LANES), axis=-1, keepdims=True)
        rest = jnp.where(lane == i1, -jnp.inf, logits)
        v2 = jnp.max(rest, axis=-1, keepdims=True)
        i2 = jnp.min(jnp.where(rest == v2, lane, LANES), axis=-1, keepdims=True)
        e2 = jnp.exp(v2 - v1)
        g1 = 1.0 / (1.0 + e2)
        g2 = e2 / (1.0 + e2)
        idx_ref[...] = jnp.where(lane == 0, i1, jnp.where(lane == 1, i2, 0))
        gate_ref[...] = jnp.where(lane == 0, g1, jnp.where(lane == 1, g2, 0.0))
    else:
        h_ref[...] = h.astype(h_ref.dtype)


def _merge(x2, z, og, ob, gn, wpa, wpb, wo, nf, wr=None):
    m = x2.shape[0]
    moe = wr is not None
    precise = wpa.dtype == F32
    tm = _tile(m, 320, 8)
    cb = D_MODEL
    row = lambda c: pl.BlockSpec((tm, cb), lambda i: (i, c))
    full = lambda a: pl.BlockSpec(a.shape, lambda i: (0,) * a.ndim)
    in_specs = [row(0), row(0), row(RA // cb), pl.BlockSpec((tm, FOX_W), lambda i: (i, 0)),
                row(GA // cb), row(GB // cb),
                full(gn), full(wpa), full(wpb), full(wo), full(nf)]
    args = [x2, og, z, ob, z, z, gn, wpa, wpb, wo, nf]
    out_specs = [row(0), row(0)]
    out_shape = [jax.ShapeDtypeStruct((m, D_MODEL), F32), jax.ShapeDtypeStruct((m, D_MODEL), F32 if (moe or precise) else BF)]
    if moe:
        in_specs.append(full(wr))
        args.append(wr)
        out_specs += [pl.BlockSpec((tm, LANES), lambda i: (i, 0))] * 2
        out_shape += [jax.ShapeDtypeStruct((m, LANES), jnp.int32), jax.ShapeDtypeStruct((m, LANES), F32)]
    return pl.pallas_call(
        functools.partial(_merge_kernel, moe=moe, precise=precise),
        grid=(m // tm,),
        in_specs=in_specs, out_specs=out_specs, out_shape=out_shape,
        compiler_params=_params("parallel"),
        name="merge",
    )(*args)


def _ffn_kernel(h_ref, x_ref, w1_ref, w3_ref, w2_ref, o_ref, acc, *, precise):
    j = pl.program_id(1)

    @pl.when(j == 0)
    def _():
        acc[...] = jnp.zeros_like(acc)

    h = h_ref[...]
    u = jax.nn.silu(_mm(h, w1_ref[...], precise)) * _mm(h, w3_ref[...], precise)
    acc[...] += _mm(u, w2_ref[...], precise)

    @pl.when(j == pl.num_programs(1) - 1)
    def _():
        o_ref[...] = x_ref[...] + acc[...]


def _ffn(h, x2, w1, w3, w2, precise):
    m = x2.shape[0]
    f = w1.shape[1]
    tm = _tile(m, 1280, 8)
    tf = _tile(f, 256, LANES)
    return pl.pallas_call(
        functools.partial(_ffn_kernel, precise=precise),
        grid=(m // tm, f // tf),
        in_specs=[
            pl.BlockSpec((tm, D_MODEL), lambda i, j: (i, 0)),
            pl.BlockSpec((tm, D_MODEL), lambda i, j: (i, 0)),
            pl.BlockSpec((D_MODEL, tf), lambda i, j: (0, j)),
            pl.BlockSpec((D_MODEL, tf), lambda i, j: (0, j)),
            pl.BlockSpec((tf, D_MODEL), lambda i, j: (j, 0)),
        ],
        out_specs=pl.BlockSpec((tm, D_MODEL), lambda i, j: (i, 0)),
        out_shape=jax.ShapeDtypeStruct((m, D_MODEL), F32),
        scratch_shapes=[pltpu.VMEM((tm, D_MODEL), F32)],
        compiler_params=_params("parallel", "arbitrary"),
        name="ffn",
    )(h, x2, w1, w3, w2)


def _moe_kernel(te_ref, tv_ref, h_ref, w1_ref, w3_ref, w2_ref, o_ref, hb):
    del te_ref
    i = pl.program_id(0)
    j = pl.program_id(1)
    live = tv_ref[i] > 0

    @pl.when(j == 0)
    def _():
        o_ref[...] = jnp.zeros_like(o_ref)
        hb[...] = h_ref[...].astype(BF)

    @pl.when(live)
    def _():
        h = hb[...]
        u = jax.nn.silu(_mm(h, w1_ref[0], False)) * _mm(h, w3_ref[0], False)
        o_ref[...] += _mm(u, w2_ref[0], False)


def _moe_experts(tile_expert, tile_valid, hs, w1, w3, w2, tm):
    rows = hs.shape[0]
    f = w1.shape[2]
    tf = _tile(f, 512, LANES)
    nf = f // tf

    def fj(i, j, tv):
        return jnp.where(tv[i] > 0, j, nf - 1)

    grid_spec = pltpu.PrefetchScalarGridSpec(
        num_scalar_prefetch=2,
        grid=(rows // tm, nf),
        in_specs=[
            pl.BlockSpec((tm, D_MODEL), lambda i, j, te, tv: (i, 0)),
            pl.BlockSpec((1, D_MODEL, tf), lambda i, j, te, tv: (te[i], 0, fj(i, j, tv))),
            pl.BlockSpec((1, D_MODEL, tf), lambda i, j, te, tv: (te[i], 0, fj(i, j, tv))),
            pl.BlockSpec((1, tf, D_MODEL), lambda i, j, te, tv: (te[i], fj(i, j, tv), 0)),
        ],
        out_specs=pl.BlockSpec((tm, D_MODEL), lambda i, j, te, tv: (i, 0)),
        scratch_shapes=[pltpu.VMEM((tm, D_MODEL), BF)],
    )
    return pl.pallas_call(
        _moe_kernel,
        grid_spec=grid_spec,
        out_shape=jax.ShapeDtypeStruct((rows, D_MODEL), F32),
        compiler_params=_params("parallel", "arbitrary"),
        name="moe_experts",
    )(tile_expert, tile_valid, hs, w1, w3, w2)


def _moe_route(idx, tm):
    mt = idx.shape[0]
    e_flat = jnp.concatenate([idx[:, 0], idx[:, 1]])
    tok = jnp.concatenate([jnp.arange(mt, dtype=jnp.int32)] * 2)
    onehot = (e_flat[:, None] == jnp.arange(N_EXPERTS, dtype=jnp.int32)[None, :]).astype(jnp.int32)
    incl = jnp.cumsum(onehot, axis=0)
    rank = jnp.sum((incl - onehot) * onehot, axis=1)
    counts = incl[-1]
    tiles_per = (counts + tm - 1) // tm
    tile_end = jnp.cumsum(tiles_per)
    start = (tile_end - tiles_per) * tm
    dest = jnp.sum(onehot * start[None, :], axis=1) + rank
    n_tiles = (2 * mt) // tm + N_EXPERTS
    slot_tok = jnp.zeros((n_tiles * tm,), jnp.int32).at[dest].set(tok)
    ti = jnp.arange(n_tiles, dtype=jnp.int32)
    tile_valid = (ti < tile_end[-1]).astype(jnp.int32)
    te = jnp.sum((ti[:, None] >= tile_end[None, :]).astype(jnp.int32), axis=1)
    last_e = jnp.sum((tile_end[-1] - 1 >= tile_end).astype(jnp.int32))
    tile_expert = jnp.where(tile_valid > 0, te, last_e).astype(jnp.int32)
    return slot_tok, tile_expert, tile_valid, dest[:mt], dest[mt:]


def _combine_kernel(*refs, final):
    if final:
        x_ref, a_ref, b_ref, g_ref, gf_ref, o_ref = refs
    else:
        x_ref, a_ref, b_ref, g_ref, o_ref = refs
    g = g_ref[...]
    y = x_ref[...] + (g[:, 0:1] * a_ref[...] + g[:, 1:2] * b_ref[...])
    o_ref[...] = _rms(y, gf_ref[...]) if final else y


def _combine(x2, a, b, gate, gfin=None):
    m = x2.shape[0]
    tm = _tile(m, 640, 8)
    spec = pl.BlockSpec((tm, D_MODEL), lambda i: (i, 0))
    in_specs = [spec] * 3 + [pl.BlockSpec((tm, LANES), lambda i: (i, 0))]
    args = [x2, a, b, gate]
    if gfin is not None:
        in_specs.append(pl.BlockSpec((1, D_MODEL), lambda i: (0, 0)))
        args.append(gfin)
    return pl.pallas_call(
        functools.partial(_combine_kernel, final=gfin is not None),
        grid=(m // tm,), in_specs=in_specs, out_specs=spec,
        out_shape=jax.ShapeDtypeStruct((m, D_MODEL), F32),
        compiler_params=_params("parallel"), name="combine",
    )(*args)


def _final_kernel(x_ref, g_ref, o_ref):
    o_ref[...] = _rms(x_ref[...], g_ref[...])


def _final_norm(x2, g):
    m = x2.shape[0]
    tm = _tile(m, 640, 8)
    spec = pl.BlockSpec((tm, D_MODEL), lambda i: (i, 0))
    return pl.pallas_call(
        _final_kernel, grid=(m // tm,),
        in_specs=[spec, pl.BlockSpec((1, D_MODEL), lambda i: (0, 0))], out_specs=spec,
        out_shape=jax.ShapeDtypeStruct((m, D_MODEL), F32),
        compiler_params=_params("parallel"), name="final_norm",
    )(x2, g)


def _pack_w_in(w):
    offs = [0]
    for s in IN_SPLITS:
        offs.append(offs[-1] + s)
    seg = [w[:, offs[i]:offs[i + 1]] for i in range(len(IN_SPLITS))]
    qa, ka, va, ra, alr, qb, kb, vb, fb, ga, gb = seg
    wm = jnp.concatenate([va, ra, ga, gb, qa, ka, qb, kb, vb], axis=1)
    ws = jnp.concatenate([fb, alr, jnp.zeros((D_MODEL, LANES - FOX_HEADS - GLA_RANK), w.dtype)], axis=1)
    wkv_t = jnp.concatenate([kb, vb], axis=1).T.astype(BF)
    return wm, ws, wkv_t


def kernel(x_prompt, x_sample, cache_k, cache_v, cache_logf, state_gla, page_table, meta_tokens, norm_mix, w_in,
           w_alpha2, b_alpha, b_f, gla_norm, w_proj_a, w_proj_b, w_out, norm_ffn, w_ffn1, w_ffn3, w_ffn2, w_router,
           w_exp1, w_exp3, w_exp2, norm_final):
    b, seq, _ = x_prompt.shape
    db, n_dec, _ = x_sample.shape
    depth = w_in.shape[0]
    n_pool = cache_k.shape[1]
    n_valid = N_META + seq
    t = -(-n_valid // BLK) * BLK
    mp, msz = b * t, db * n_dec

    xp = jnp.concatenate([jnp.broadcast_to(meta_tokens[None].astype(F32), (b, N_META, D_MODEL)), x_prompt,
                          jnp.zeros((b, t - n_valid, D_MODEL), F32)], axis=1).reshape(mp, D_MODEL)
    xs = x_sample.reshape(msz, D_MODEL)
    cache_kt = jnp.transpose(cache_k, (0, 1, 3, 4, 2)).reshape(depth, n_pool, FOX_W, BLK)
    cache_vt = jnp.transpose(cache_v, (0, 1, 3, 4, 2)).reshape(depth, n_pool, FOX_W, BLK)
    page_rows = depth * n_pool * FOX_HEADS
    rows_per = _tile(page_rows, 2048, 8)
    lf_pages = jnp.transpose(cache_logf, (0, 1, 3, 2)).reshape(page_rows // rows_per, rows_per, BLK)
    cpage = _cumsum_lanes(lf_pages, running=False).reshape(depth, n_pool, FOX_HEADS, BLK)
    page_flat = page_table.reshape(-1).astype(jnp.int32)

    outs = {k: [] for k in ("kp", "vp", "lp", "sp", "ks", "vs", "ls", "ss")}
    y_prompt = y_sample = None
    for l in range(depth):
        wm32, ws32, wkv_t = _pack_w_in(w_in[l])
        wm, ws = wm32[:, :NZ_PROMPT].astype(BF), ws32.astype(BF)
        wst = ws.T
        wa32 = jnp.zeros((LANES, GLA_QK), F32).at[FOX_HEADS:FOX_HEADS + GLA_RANK].set(w_alpha2[l])
        wa = wa32.astype(BF)
        ba = b_alpha[l].reshape(1, GLA_QK)
        bfp = jnp.zeros((1, LANES), F32).at[0, :FOX_HEADS].set(b_f[l])
        bfc = b_f[l].reshape(FOX_HEADS, 1)
        g_mix = norm_mix[l].reshape(1, D_MODEL)
        gn = gla_norm[l].reshape(1, GLA_DV)
        wpa, wpb, wo = w_proj_a[l].astype(BF), w_proj_b[l].astype(BF), w_out[l].astype(BF)
        nf = norm_ffn[l].reshape(1, D_MODEL)
        is_moe = l % 2 == 1
        last = l == depth - 1
        jx = l // 2

        z_p, la_p, kt_p, vt_p, lft_p = _inproj(xp, g_mix, wm, ws, wa, ba, bfp,
                                               head_major=(b, wkv_t, wst, bfc))
        z3 = z_p.reshape(b, t, NZ_PROMPT)
        c_rows = _cumsum_lanes(lft_p, running=True)
        ob_p = _fox_prompt(z3, kt_p, vt_p, c_rows.reshape(b, FOX_HEADS // 2, 2, t))
        og_p, s_p = _gla_prompt(z3, la_p.reshape(b, t, GLA_QK), n_valid)
        z_s, la_s, lf_s = _inproj(xs, g_mix, wm32, ws32, wa32, ba, bfp)
        zs3 = z_s.reshape(db, n_dec, NZ)
        og_s, s_s = _gla_sample(zs3, la_s.reshape(db, n_dec, GLA_QK), state_gla[l])
        kn3 = zs3[:, :, KB:KB + FOX_W]
        vn3 = zs3[:, :, VB:VB + FOX_W]
        ob_s = _fox_sample(page_flat, cache_kt, cache_vt, cpage, l, zs3, kn3, vn3,
                           lf_s.reshape(db, n_dec, LANES))

        outs["kp"].append(kt_p[..., :n_valid])
        outs["vp"].append(vt_p[..., :n_valid])
        outs["lp"].append(lft_p[..., :n_valid])
        outs["sp"].append(s_p)
        outs["ks"].append(kn3.reshape(db, n_dec, FOX_HEADS, FOX_HD))
        outs["vs"].append(vn3.reshape(db, n_dec, FOX_HEADS, FOX_HD))
        outs["ls"].append(lf_s.reshape(db, n_dec, LANES)[:, :, :FOX_HEADS])
        outs["ss"].append(s_s)

        og_p2, ob_p2 = og_p.reshape(mp, GLA_V), ob_p.reshape(mp, FOX_W)
        og_s2, ob_s2 = og_s.reshape(msz, GLA_V), ob_s.reshape(msz, FOX_W)
        if not is_moe:
            xp, h_p = _merge(xp, z_p, og_p2, ob_p2, gn, wpa, wpb, wo, nf)
            xs, h_s = _merge(xs, z_s, og_s2, ob_s2, gn, w_proj_a[l], w_proj_b[l], w_out[l], nf)
            xs = _ffn(h_s, xs, w_ffn1[jx], w_ffn3[jx], w_ffn2[jx], True)
            xp = _ffn(h_p, xp, w_ffn1[jx], w_ffn3[jx], w_ffn2[jx], False)
        else:
            wr = jnp.zeros((D_MODEL, LANES), F32).at[:, :N_EXPERTS].set(w_router[jx])
            wr1 = wr.astype(BF)
            wr = jnp.stack([wr1, (wr - wr1.astype(F32)).astype(BF)])
            xp, h_p, idx_p, gate_p = _merge(xp, z_p, og_p2, ob_p2, gn, wpa, wpb, wo, nf, wr)
            xs, h_s, idx_s, gate_s = _merge(xs, z_s, og_s2, ob_s2, gn, w_proj_a[l], w_proj_b[l], w_out[l], nf, wr)
            h_all = jnp.concatenate([h_p, h_s], axis=0)
            idx = jnp.concatenate([idx_p[:, :2], idx_s[:, :2]], axis=0)
            tm = 1024 if 2 * (mp + msz) >= 8 * 1024 else 256
            slot_tok, tile_e, tile_v, d1, d2 = _moe_route(idx, tm)
            rows = lambda a, i: a.at[i].get(mode="promise_in_bounds")
            hs = rows(h_all, slot_tok)
            ys = _moe_experts(tile_e, tile_v, hs, w_exp1[jx], w_exp3[jx], w_exp2[jx], tm)
            gfin = norm_final.reshape(1, D_MODEL) if last else None
            yp = _combine(xp, rows(ys, d1[:mp]), rows(ys, d2[:mp]), gate_p, gfin)
            ysm = _combine(xs, rows(ys, d1[mp:]), rows(ys, d2[mp:]), gate_s, gfin)
            if last:
                y_prompt, y_sample = yp, ysm
            else:
                xp, xs = yp, ysm

    if y_prompt is None:
        g_fin = norm_final.reshape(1, D_MODEL)
        y_prompt, y_sample = _final_norm(xp, g_fin), _final_norm(xs, g_fin)
    y_prompt = y_prompt.reshape(b, t, D_MODEL)[:, N_META:n_valid]
    y_sample = y_sample.reshape(db, n_dec, D_MODEL)
    st = jnp.stack
    return (y_prompt, y_sample,
            jnp.transpose(st(outs["kp"]), (0, 1, 4, 2, 3)), jnp.transpose(st(outs["vp"]), (0, 1, 4, 2, 3)),
            jnp.transpose(st(outs["lp"]), (0, 1, 3, 2)), st(outs["sp"]),
            st(outs["ks"]), st(outs["vs"]), st(outs["ls"]), st(outs["ss"]))
```

```python
import functools

import jax
import jax.numpy as jnp
from jax import lax
from jax.experimental import pallas as pl
from jax.experimental.pallas import tpu as pltpu

BF = jnp.bfloat16
F32 = jnp.float32

D_MODEL = 1024
N_META = 16
BLK = 128
GLA_HEADS = 4
GLA_DK = 128
GLA_DV = 256
GLA_RANK = 16
GLA_TAU = 16.0
GLA_SUB = 16
FOX_HEADS = 8
FOX_HD = 64
N_EXPERTS = 8
MOE_CHUNKS = 4
RMS_EPS = 1e-6
NEG = -1e30
LANES = 128

GLA_QK = GLA_HEADS * GLA_DK
GLA_V = GLA_HEADS * GLA_DV
FOX_W = FOX_HEADS * FOX_HD
VA, RA, GA, GB = 0, 1024, 2048, 3072
QA, KA = 4096, 4608
QB, KB, VB = 5120, 5632, 6144
NZ_PROMPT = 5632
NZ = 6656
IN_SPLITS = (GLA_QK, GLA_QK, GLA_V, GLA_V, GLA_RANK, FOX_W, FOX_W, FOX_W, FOX_HEADS, D_MODEL, D_MODEL)

VMEM_LIMIT = 56 * 1024 * 1024


def _params(*sem):
    return pltpu.CompilerParams(dimension_semantics=sem, vmem_limit_bytes=VMEM_LIMIT)


def _tile(n, target, mult):
    best = None
    for t in range(mult, min(n, target) + 1, mult):
        if n % t == 0:
            best = t
    return best if best is not None else n


def _dot(a, b):
    return jnp.dot(a, b, preferred_element_type=F32)


def _dot_nt(a, b):
    return lax.dot_general(a, b, (((1,), (1,)), ((), ())), preferred_element_type=F32)


def _split2(x):
    hi = x.astype(BF)
    return hi, (x - hi.astype(F32)).astype(BF)


def _dot3(a, b, nt=False):
    dot = _dot_nt if nt else _dot
    ah, al = _split2(a)
    bh, bl = _split2(b)
    return (dot(al, bh) + dot(ah, bl)) + dot(ah, bh)


def _mm(a, w, precise):
    return _dot3(a, w) if precise else _dot(a.astype(BF), w.astype(BF))


def _split3(x):
    h1 = x.astype(BF)
    r1 = x - h1.astype(F32)
    h2 = r1.astype(BF)
    r2 = r1 - h2.astype(F32)
    return h1, h2, r2.astype(BF)


def _dot_exact_r(x, m):
    h1, h2, h3 = _split3(x)
    return (_dot(h3, m) + _dot(h2, m)) + _dot(h1, m)


def _dot_exact_l(m, x):
    h1, h2, h3 = _split3(x)
    return (_dot(m, h3) + _dot(m, h2)) + _dot(m, h1)


def _logsig(x):
    return jnp.minimum(x, 0.0) - jnp.log1p(jnp.exp(-jnp.abs(x)))


def _rms(x, g):
    ms = jnp.mean(x * x, axis=-1, keepdims=True)
    return x * lax.rsqrt(ms + RMS_EPS) * g


def _inproj_kernel(*refs, head_major):
    if head_major:
        (x_ref, g_ref, wm_ref, ws_ref, wa_ref, ba_ref, bf_ref, wkv_ref, wst_ref, bfc_ref,
         z_ref, la_ref, kt_ref, vt_ref, lft_ref, h_scr) = refs
    else:
        x_ref, g_ref, wm_ref, ws_ref, wa_ref, ba_ref, bf_ref, z_ref, la_ref, lf_ref, h_scr = refs

    precise = not head_major

    @pl.when(pl.program_id(1) == 0)
    def _():
        hf = _rms(x_ref[...], g_ref[...])
        h, h_lo = _split2(hf)
        h_scr[0] = h
        h_scr[1] = h_lo
        s = _mm(hf, ws_ref[...], precise)
        a = _mm(s, wa_ref[...], precise) + ba_ref[...]
        la_ref[...] = _logsig(a) * (1.0 / GLA_TAU)
        if head_major:
            tm = h.shape[0]
            kv = _dot_nt(wkv_ref[...], h)
            kt_ref[0] = kv[:FOX_W].reshape(FOX_HEADS, FOX_HD, tm)
            vt_ref[0] = kv[FOX_W:].reshape(FOX_HEADS, FOX_HD, tm)
            st = _dot_nt(wst_ref[...], h)
            lft_ref[0] = _logsig(st[:FOX_HEADS] + bfc_ref[...])
        else:
            lf_ref[...] = _logsig(s + bf_ref[...])

    if precise:
        wh, wl = _split2(wm_ref[...])
        z_ref[...] = (_dot(h_scr[1], wh) + _dot(h_scr[0], wl)) + _dot(h_scr[0], wh)
    else:
        z_ref[...] = _dot(h_scr[0], wm_ref[...])


def _inproj(x2, g, wm, ws, wa, ba, bfp, head_major=None):
    m = x2.shape[0]
    nz = wm.shape[1]
    const = lambda shape: pl.BlockSpec(shape, lambda i, j: (0,) * len(shape))
    if head_major is not None:
        b, wkv, wst, bfc = head_major
        t = m // b
        tm = _tile(t, 640, BLK)
        nt = t // tm
    else:
        tm = _tile(m, 1280, 8)
    tn = _tile(nz, 1664, LANES)
    in_specs = [
        pl.BlockSpec((tm, D_MODEL), lambda i, j: (i, 0)),
        const((1, D_MODEL)),
        pl.BlockSpec((D_MODEL, tn), lambda i, j: (0, j)),
        const((D_MODEL, LANES)), const((LANES, GLA_QK)), const((1, GLA_QK)), const((1, LANES)),
    ]
    args = [x2, g, wm, ws, wa, ba, bfp]
    out_specs = [pl.BlockSpec((tm, tn), lambda i, j: (i, j)),
                 pl.BlockSpec((tm, GLA_QK), lambda i, j: (i, 0))]
    out_shape = [jax.ShapeDtypeStruct((m, nz), F32), jax.ShapeDtypeStruct((m, GLA_QK), F32)]
    if head_major is not None:
        in_specs += [const((2 * FOX_W, D_MODEL)), const((LANES, D_MODEL)), const((FOX_HEADS, 1))]
        args += [wkv, wst, bfc]
        hm = pl.BlockSpec((1, FOX_HEADS, FOX_HD, tm), lambda i, j: (i // nt, 0, 0, i % nt))
        out_specs += [hm, hm, pl.BlockSpec((1, FOX_HEADS, tm), lambda i, j: (i // nt, 0, i % nt))]
        out_shape += [jax.ShapeDtypeStruct((b, FOX_HEADS, FOX_HD, t), F32)] * 2
        out_shape += [jax.ShapeDtypeStruct((b, FOX_HEADS, t), F32)]
    else:
        out_specs.append(pl.BlockSpec((tm, LANES), lambda i, j: (i, 0)))
        out_shape.append(jax.ShapeDtypeStruct((m, LANES), F32))
    return pl.pallas_call(
        functools.partial(_inproj_kernel, head_major=head_major is not None),
        grid=(m // tm, nz // tn),
        in_specs=in_specs, out_specs=out_specs, out_shape=out_shape,
        scratch_shapes=[pltpu.VMEM((2, tm, D_MODEL), BF)],
        compiler_params=_params("parallel", "arbitrary"),
        name="inproj",
    )(*args)


def _cumsum_kernel(x_ref, c_ref, carry, *, nsub, running):
    if running:
        @pl.when(pl.program_id(1) == 0)
        def _():
            carry[...] = jnp.zeros_like(carry)

    r = lax.broadcasted_iota(jnp.int32, (BLK, BLK), 0)
    c = lax.broadcasted_iota(jnp.int32, (BLK, BLK), 1)
    upper = (r <= c).astype(BF)
    for u in range(nsub):
        cb = _dot_exact_r(x_ref[0, :, u * BLK:(u + 1) * BLK], upper)
        if running:
            cb = cb + carry[...]
            carry[...] = cb[:, BLK - 1:BLK]
        c_ref[0, :, u * BLK:(u + 1) * BLK] = cb


def _cumsum_lanes(x3, running):
    a, rows, t = x3.shape
    tb = _tile(t, 640, BLK)
    return pl.pallas_call(
        functools.partial(_cumsum_kernel, nsub=tb // BLK, running=running),
        grid=(a, t // tb),
        in_specs=[pl.BlockSpec((1, rows, tb), lambda i, j: (i, 0, j))],
        out_specs=pl.BlockSpec((1, rows, tb), lambda i, j: (i, 0, j)),
        out_shape=jax.ShapeDtypeStruct(x3.shape, F32),
        scratch_shapes=[pltpu.VMEM((rows, 1), F32)],
        compiler_params=_params("parallel", "arbitrary"),
        name="cumsum_lanes",
    )(x3)


def _fox_prompt_kernel(q_ref, k_ref, v_ref, ck_ref, o_ref, ka, va, m_scr, acc_scr, *, tq):
    qi = pl.program_id(2)
    t = ka.shape[2]

    @pl.when(qi == 0)
    def _():
        row = lax.broadcasted_iota(jnp.int32, (FOX_HD, t), 0)
        for h in range(2):
            c1, c2, c3 = _split3(-ck_ref[0, 0, h:h + 1, :])
            extra = jnp.where(row == 0, c1.astype(F32), jnp.where(row == 1, c2.astype(F32),
                              jnp.where(row == 2, c3.astype(F32), 0.0)))
            ka[h] = jnp.concatenate([k_ref[0, h], extra], axis=0).astype(BF)
            va[h] = jnp.concatenate([v_ref[0, h], jnp.where(row == 0, 1.0, 0.0)], axis=0).astype(BF)

    lane = lax.broadcasted_iota(jnp.int32, (1, LANES), 1)
    ones3 = jnp.where((lane >= FOX_HD) & (lane < FOX_HD + 3), 1.0, 0.0)
    q2 = q_ref[0] * (FOX_HD ** -0.5)
    qh = (jnp.where(lane < FOX_HD, q2, ones3).astype(BF),
          jnp.where(lane < FOX_HD, pltpu.roll(q2, FOX_HD, 1), ones3).astype(BF))
    m_scr[...] = jnp.full(m_scr.shape, NEG, F32)
    acc_scr[...] = jnp.zeros(acc_scr.shape, F32)

    def logits(j):
        start = pl.multiple_of(j * tq, tq)
        return tuple(_dot(qh[h], ka[h, :, pl.ds(start, tq)]) for h in range(2))

    def attend(j, ss, causal):
        start = pl.multiple_of(j * tq, tq)
        for h in range(2):
            s = ss[h]
            if causal:
                ri = lax.broadcasted_iota(jnp.int32, (tq, tq), 0)
                ci = lax.broadcasted_iota(jnp.int32, (tq, tq), 1)
                s = jnp.where(ci <= ri, s, NEG)
            m_old = m_scr[h]
            m_new = jnp.maximum(m_old, jnp.max(s, axis=-1, keepdims=True))
            p = jnp.exp(s - m_new).astype(BF)
            acc_scr[h] = jnp.exp(m_old - m_new) * acc_scr[h] + _dot_nt(p, va[h, :, pl.ds(start, tq)])
            m_scr[h] = m_new

    def tiles(*js_causal):
        ss = [logits(j) for j, _ in js_causal]
        for (j, causal), s in zip(js_causal, ss):
            attend(j, s, causal)

    def body(j2, carry):
        tiles((2 * j2, False), (2 * j2 + 1, False))
        return carry

    lax.fori_loop(0, qi // 2, body, 0)

    @pl.when((qi & 1) == 1)
    def _():
        tiles((qi - 1, False), (qi, True))

    @pl.when((qi & 1) == 0)
    def _():
        tiles((qi, True))
    o0 = acc_scr[0]
    o1 = acc_scr[1]
    o0 = o0 / o0[:, FOX_HD:FOX_HD + 1]
    o1 = o1 / o1[:, FOX_HD:FOX_HD + 1]
    o_ref[0] = jnp.where(lane < FOX_HD, o0, pltpu.roll(o1, FOX_HD, 1))


def _fox_prompt(z3, kt, vt, ck4):
    b, t, _ = z3.shape
    tq = _tile(t, 640, BLK)
    npair = FOX_HEADS // 2
    kv_spec = pl.BlockSpec((1, 2, FOX_HD, t), lambda i, hp, qi: (i, hp, 0, 0))
    return pl.pallas_call(
        functools.partial(_fox_prompt_kernel, tq=tq),
        grid=(b, npair, t // tq),
        in_specs=[
            pl.BlockSpec((1, tq, LANES), lambda i, hp, qi: (i, qi, QB // LANES + hp)),
            kv_spec, kv_spec,
            pl.BlockSpec((1, 1, 2, t), lambda i, hp, qi: (i, hp, 0, 0)),
        ],
        out_specs=pl.BlockSpec((1, tq, LANES), lambda i, hp, qi: (i, qi, hp)),
        out_shape=jax.ShapeDtypeStruct((b, t, FOX_W), F32),
        scratch_shapes=[
            pltpu.VMEM((2, 2 * FOX_HD, t), BF), pltpu.VMEM((2, 2 * FOX_HD, t), BF),
            pltpu.VMEM((2, tq, 1), F32), pltpu.VMEM((2, tq, LANES), F32),
        ],
        compiler_params=_params("parallel", "parallel", "arbitrary"),
        name="fox_prompt",
    )(z3, kt, vt, ck4)


def _gla_masks():
    r = jnp.arange(BLK)[:, None]
    c = jnp.arange(BLK)[None, :]
    masks = [(c < GLA_SUB) & ((r % GLA_SUB) + c < GLA_SUB)]
    m = BLK // 2
    while m >= GLA_SUB:
        masks.append((r // (2 * m) == c // (2 * m)) & (r % (2 * m) >= m) & (c % (2 * m) < m))
        m //= 2
    k = jnp.arange(GLA_SUB * GLA_DK)[:, None]
    collect = (k // GLA_DK == c).astype(BF)
    return jnp.stack(masks).astype(F32), collect


def _gla_chunk_math(qs, ks, vs, las, s0s, msk_ref, col_ref, pq, pc, prod):
    heads = range(len(qs))
    ri = lax.broadcasted_iota(jnp.int32, (BLK, BLK), 0)
    ci = lax.broadcasted_iota(jnp.int32, (BLK, BLK), 1)
    lower = (ri >= ci).astype(BF)
    cums = [_dot_exact_l(lower, las[h]) for h in heads]
    for h in heads:
        pq[h, 0:BLK, :] = qs[h]
        pc[h, 0:BLK, :] = cums[h]
        pq[h, BLK:, :] = jnp.zeros((GLA_SUB, GLA_DK), F32)
        pc[h, BLK:, :] = jnp.zeros((GLA_SUB, GLA_DK), F32)
    for h in heads:
        for d in range(GLA_SUB):
            e = jnp.exp(jnp.minimum(pc[h, d:d + BLK, :] - cums[h], 0.0))
            prod[h, :, d * GLA_DK:(d + 1) * GLA_DK] = (pq[h, d:d + BLK, :] * ks[h] * e).astype(BF)
    bands = [_dot(prod[h], col_ref[...]) for h in heads]
    a = [pltpu.roll(jnp.where(msk_ref[0] > 0.5, bands[h], 0.0), 0, 1, stride=1, stride_axis=0).T
         for h in heads]
    m = BLK // 2
    lvl = 1
    while m >= GLA_SUB:
        nb = BLK // (2 * m)
        up = (ri & (2 * m - 1)) >= m
        for h in heads:
            cum = cums[h]
            pieces = [jnp.broadcast_to(cum[2 * m * u + m - 1:2 * m * u + m, :], (2 * m, GLA_DK)) for u in range(nb)]
            bnd = pieces[0] if nb == 1 else jnp.concatenate(pieces, axis=0)
            dq = cum - bnd
            qt = jnp.where(up, qs[h] * jnp.exp(jnp.minimum(dq, 0.0)), 0.0).astype(BF)
            kt = jnp.where(up, 0.0, ks[h] * jnp.exp(jnp.minimum(-dq, 0.0))).astype(BF)
            a[h] = jnp.where(msk_ref[lvl] > 0.5, _dot_nt(qt, kt), a[h])
        m //= 2
        lvl += 1
    outs = []
    for h in heads:
        cum = cums[h]
        vb = vs[h].astype(BF)
        o = _dot(a[h].astype(BF), vb) + _dot((qs[h] * jnp.exp(cum)).astype(BF), s0s[h].astype(BF))
        last = cum[BLK - 1:BLK, :]
        kd_t = (ks[h] * jnp.exp(last - cum)).T.astype(BF)
        last_col = jnp.sum(jnp.where(ri == ci, last, 0.0), axis=1, keepdims=True)
        outs.append((o, jnp.exp(last_col) * s0s[h] + _dot(kd_t, vb)))
    return outs


def _gla_prompt_kernel(q_ref, k_ref, v_ref, la_ref, msk_ref, col_ref, o_ref, s_ref, s_scr, pq, pc, prod, *, n_valid):
    c = pl.program_id(1)

    @pl.when(c == 0)
    def _():
        s_scr[...] = jnp.zeros_like(s_scr)

    real = (c * BLK + lax.broadcasted_iota(jnp.int32, (BLK, GLA_DK), 0)) < n_valid
    dk = lambda h: slice(h * GLA_DK, (h + 1) * GLA_DK)
    dv = lambda h: slice(h * GLA_DV, (h + 1) * GLA_DV)
    heads = range(GLA_HEADS)
    qs = [q_ref[0, :, dk(h)] * (GLA_DK ** -0.5) for h in heads]
    ks = [jnp.where(real, k_ref[0, :, dk(h)], 0.0) for h in heads]
    las = [jnp.where(real, la_ref[0, :, dk(h)], 0.0) for h in heads]
    vs = [v_ref[0, :, dv(h)] for h in heads]
    outs = _gla_chunk_math(qs, ks, vs, las, [s_scr[h] for h in heads], msk_ref, col_ref, pq, pc, prod)
    for h in heads:
        o, s1 = outs[h]
        o_ref[0, :, dv(h)] = o
        s_scr[h] = s1
        s_ref[0, h] = s1


def _gla_prompt(z3, la3, n_valid):
    b, t, _ = z3.shape
    masks, collect = _gla_masks()
    return pl.pallas_call(
        functools.partial(_gla_prompt_kernel, n_valid=n_valid),
        grid=(b, t // BLK),
        in_specs=[
            pl.BlockSpec((1, BLK, GLA_QK), lambda i, c: (i, c, QA // GLA_QK)),
            pl.BlockSpec((1, BLK, GLA_QK), lambda i, c: (i, c, KA // GLA_QK)),
            pl.BlockSpec((1, BLK, GLA_V), lambda i, c: (i, c, VA // GLA_V)),
            pl.BlockSpec((1, BLK, GLA_QK), lambda i, c: (i, c, 0)),
            pl.BlockSpec(masks.shape, lambda i, c: (0, 0, 0)),
            pl.BlockSpec(collect.shape, lambda i, c: (0, 0)),
        ],
        out_specs=[
            pl.BlockSpec((1, BLK, GLA_V), lambda i, c: (i, c, 0)),
            pl.BlockSpec((1, GLA_HEADS, GLA_DK, GLA_DV), lambda i, c: (i, 0, 0, 0)),
        ],
        out_shape=[
            jax.ShapeDtypeStruct((b, t, GLA_V), F32),
            jax.ShapeDtypeStruct((b, GLA_HEADS, GLA_DK, GLA_DV), F32),
        ],
        scratch_shapes=[pltpu.VMEM((GLA_HEADS, GLA_DK, GLA_DV), F32),
                        pltpu.VMEM((GLA_HEADS, BLK + GLA_SUB, GLA_DK), F32),
                        pltpu.VMEM((GLA_HEADS, BLK + GLA_SUB, GLA_DK), F32),
                        pltpu.VMEM((GLA_HEADS, BLK, GLA_SUB * GLA_DK), BF)],
        compiler_params=_params("parallel", "arbitrary"),
        name="gla_prompt",
    )(z3, z3, z3, la3, masks, collect)


def _gla_sample_kernel(q_ref, k_ref, v_ref, la_ref, s0_ref, o_ref, s1_ref, pq, pk, pc, pv, *, n):
    ri = lax.broadcasted_iota(jnp.int32, (n, GLA_DK), 0)
    for h in range(GLA_HEADS):
        q = q_ref[0, :, h * GLA_DK:(h + 1) * GLA_DK] * (GLA_DK ** -0.5)
        k = k_ref[0, :, h * GLA_DK:(h + 1) * GLA_DK]
        la = la_ref[0, :, h * GLA_DK:(h + 1) * GLA_DK]
        v = v_ref[0, :, h * GLA_DV:(h + 1) * GLA_DV]
        s0 = s0_ref[0, h]
        cum = jnp.zeros_like(la)
        for j in range(n):
            cum = cum + jnp.where(ri >= j, la[j:j + 1, :], 0.0)
        o = jnp.zeros((n, GLA_DV), F32)
        for s in range(n):
            w = q * jnp.exp(jnp.minimum(cum - cum[s:s + 1, :], 0.0)) * k[s:s + 1, :]
            sc = jnp.sum(w, axis=-1, keepdims=True)
            o = o + jnp.where(ri[:, :1] >= s, sc, 0.0) * v[s:s + 1, :]
        for scr, val in ((pq, q * jnp.exp(cum)), (pk, k), (pc, cum)):
            scr[...] = jnp.zeros_like(scr)
            scr[0:n, :] = val
        pv[...] = jnp.zeros_like(pv)
        pv[0:n, :] = v
        o_inter = _dot3(pq[...], s0)
        o_ref[0, :, h * GLA_DV:(h + 1) * GLA_DV] = o + o_inter[0:n, :]
        cum_t = pc[...].T
        k_t = pk[...].T
        last = cum_t[:, n - 1:n]
        kd = k_t * jnp.exp(jnp.minimum(last - cum_t, 0.0))
        s1_ref[0, h] = jnp.exp(last) * s0 + _dot3(kd, pv[...])


def _gla_sample(z3, la3, state):
    db, n, _ = z3.shape
    return pl.pallas_call(
        functools.partial(_gla_sample_kernel, n=n),
        grid=(db,),
        in_specs=[
            pl.BlockSpec((1, n, GLA_QK), lambda i: (i, 0, QA // GLA_QK)),
            pl.BlockSpec((1, n, GLA_QK), lambda i: (i, 0, KA // GLA_QK)),
            pl.BlockSpec((1, n, GLA_V), lambda i: (i, 0, VA // GLA_V)),
            pl.BlockSpec((1, n, GLA_QK), lambda i: (i, 0, 0)),
            pl.BlockSpec((1, GLA_HEADS, GLA_DK, GLA_DV), lambda i: (i, 0, 0, 0)),
        ],
        out_specs=[
            pl.BlockSpec((1, n, GLA_V), lambda i: (i, 0, 0)),
            pl.BlockSpec((1, GLA_HEADS, GLA_DK, GLA_DV), lambda i: (i, 0, 0, 0)),
        ],
        out_shape=[
            jax.ShapeDtypeStruct((db, n, GLA_V), F32),
            jax.ShapeDtypeStruct(state.shape, F32),
        ],
        scratch_shapes=[pltpu.VMEM((BLK, GLA_DK), F32), pltpu.VMEM((BLK, GLA_DK), F32),
                        pltpu.VMEM((BLK, GLA_DK), F32), pltpu.VMEM((BLK, GLA_DV), F32)],
        compiler_params=_params("parallel"),
        name="gla_sample",
    )(z3, z3, z3, la3, state)


def _fox_sample_kernel(pt_ref, *refs, pps, n):
    del pt_ref
    k_refs = refs[0:pps]
    v_refs = refs[pps:2 * pps]
    c_refs = refs[2 * pps:3 * pps]
    q_ref, kn_ref, vn_ref, lfn_ref, o_ref, qbd, carry, m_scr, l_scr, acc_scr = refs[3 * pps:]
    j = pl.program_id(1)
    rows = n * FOX_HEADS
    rr = lax.broadcasted_iota(jnp.int32, (rows, FOX_W), 0)
    cc = lax.broadcasted_iota(jnp.int32, (rows, FOX_W), 1)
    own = (cc >> 6) == (rr & (FOX_HEADS - 1))

    @pl.when(j == 0)
    def _():
        qrep = jnp.concatenate([jnp.broadcast_to(q_ref[0, t:t + 1, :], (FOX_HEADS, FOX_W)) for t in range(n)], axis=0)
        qbd[...] = jnp.where(own, qrep * (FOX_HD ** -0.5), 0.0)
        carry[...] = jnp.zeros_like(carry)
        m_scr[...] = jnp.full(m_scr.shape, NEG, F32)
        l_scr[...] = jnp.zeros_like(l_scr)
        acc_scr[...] = jnp.zeros_like(acc_scr)

    qh, ql = _split2(qbd[...])
    qq = jnp.concatenate([qh, ql], axis=0)
    run = carry[...]
    ss = []
    for r in range(pps):
        cg = run + c_refs[r][0, 0]
        run = cg[:, BLK - 1:BLK]
        kh, kl = _split2(k_refs[r][0, 0])
        t = _dot(qq, kh)
        ss.append(((t[rows:] + _dot(qh, kl)) + t[:rows]) - jnp.concatenate([cg] * n, axis=0))
    carry[...] = run
    s = jnp.concatenate(ss, axis=1)
    m_old = m_scr[...]
    m_new = jnp.maximum(m_old, jnp.max(s, axis=-1, keepdims=True))
    p = jnp.exp(s - m_new)
    alpha = jnp.exp(m_old - m_new)
    l_scr[...] = alpha * l_scr[...] + jnp.sum(p, axis=-1, keepdims=True)
    ph, pl_ = _split2(p)
    pv = None
    for r in range(pps):
        cols = slice(r * BLK, (r + 1) * BLK)
        vh, vl = _split2(v_refs[r][0, 0])
        t = _dot_nt(jnp.concatenate([ph[:, cols], pl_[:, cols]], axis=0), vh)
        t3 = (t[rows:] + _dot_nt(ph[:, cols], vl)) + t[:rows]
        pv = t3 if pv is None else pv + t3
    acc_scr[...] = alpha * acc_scr[...] + pv
    m_scr[...] = m_new

    @pl.when(j == pl.num_programs(1) - 1)
    def _():
        lfn = lfn_ref[0]
        ri = lax.broadcasted_iota(jnp.int32, (n, LANES), 0)
        cn = jnp.zeros_like(lfn)
        for i in range(n):
            cn = cn + jnp.where(ri >= i, lfn[i:i + 1, :], 0.0)
        er = lax.broadcasted_iota(jnp.int32, (rows, LANES), 0)
        ec = lax.broadcasted_iota(jnp.int32, (rows, LANES), 1)
        sel = (er & (FOX_HEADS - 1)) == ec
        tot = jnp.concatenate([carry[...]] * n, axis=0)
        qf = qbd[...]
        tok = rr[:, :1] >> 3
        m_run, l_run, acc = m_scr[...], l_scr[...], acc_scr[...]
        for i in range(n):
            ci = jnp.sum(jnp.where(sel, cn[i:i + 1, :], 0.0), axis=-1, keepdims=True)
            s = jnp.sum(qf * kn_ref[0, i:i + 1, :], axis=-1, keepdims=True) - (tot + ci)
            s = jnp.where(tok >= i, s, NEG)
            m_new = jnp.maximum(m_run, s)
            p = jnp.exp(s - m_new)
            alpha = jnp.exp(m_run - m_new)
            l_run = alpha * l_run + p
            acc = alpha * acc + p * vn_ref[0, i:i + 1, :]
            m_run = m_new
        o_full = jnp.where(own, acc / l_run, 0.0)
        o_ref[0] = jnp.sum(o_full.reshape(n, FOX_HEADS, FOX_W), axis=1)


def _fox_sample(page_flat, cache_kt, cache_vt, cpage, layer, q3, kn3, vn3, lfn3):
    db, n, _ = q3.shape
    n_pages = page_flat.shape[0] // db
    pps = _tile(n_pages, 8, 1)
    rows = n * FOX_HEADS

    def page_map(r):
        return lambda i, j, pt: (layer, pt[i * n_pages + j * pps + r], 0, 0)

    def kv_specs():
        return [pl.BlockSpec((1, 1, FOX_W, BLK), page_map(r)) for r in range(pps)]

    c_specs = [pl.BlockSpec((1, 1, FOX_HEADS, BLK), page_map(r)) for r in range(pps)]
    grid_spec = pltpu.PrefetchScalarGridSpec(
        num_scalar_prefetch=1,
        grid=(db, n_pages // pps),
        in_specs=kv_specs() + kv_specs() + c_specs + [
            pl.BlockSpec((1, n, FOX_W), lambda i, j, pt: (i, 0, QB // FOX_W)),
            pl.BlockSpec((1, n, FOX_W), lambda i, j, pt: (i, 0, 0)),
            pl.BlockSpec((1, n, FOX_W), lambda i, j, pt: (i, 0, 0)),
            pl.BlockSpec((1, n, LANES), lambda i, j, pt: (i, 0, 0)),
        ],
        out_specs=pl.BlockSpec((1, n, FOX_W), lambda i, j, pt: (i, 0, 0)),
        scratch_shapes=[
            pltpu.VMEM((rows, FOX_W), F32), pltpu.VMEM((FOX_HEADS, 1), F32),
            pltpu.VMEM((rows, 1), F32), pltpu.VMEM((rows, 1), F32), pltpu.VMEM((rows, FOX_W), F32),
        ],
    )
    return pl.pallas_call(
        functools.partial(_fox_sample_kernel, pps=pps, n=n),
        grid_spec=grid_spec,
        out_shape=jax.ShapeDtypeStruct((db, n, FOX_W), F32),
        compiler_params=_params("parallel", "arbitrary"),
        name="fox_sample",
    )(page_flat, *([cache_kt] * pps), *([cache_vt] * pps), *([cpage] * pps), q3, kn3, vn3, lfn3)


def _merge_kernel(*refs, moe, precise):
    (x_ref, og_ref, r_ref, ob_ref, ga_ref, gb_ref, gn_ref, wpa_ref, wpb_ref, wo_ref, nf_ref) = refs[:11]
    if moe:
        wr_ref, xo_ref, h_ref, idx_ref, gate_ref = refs[11:]
    else:
        xo_ref, h_ref = refs[11:]
    og = og_ref[...]
    gn = gn_ref[...]
    oa = jnp.concatenate([_rms(og[:, h * GLA_DV:(h + 1) * GLA_DV], gn) for h in range(GLA_HEADS)], axis=-1)
    oa = oa * jax.nn.silu(r_ref[...])
    pa = _mm(oa, wpa_ref[...], precise)
    pb = _mm(ob_ref[...], wpb_ref[...], precise)
    y = jax.nn.sigmoid(ga_ref[...]) * pa + jax.nn.sigmoid(gb_ref[...]) * pb
    xn = x_ref[...] + _mm(y, wo_ref[...], precise)
    xo_ref[...] = xn
    h = _rms(xn, nf_ref[...])
    if moe:
        h_ref[...] = h
        h1 = h.astype(BF)
        h2 = (h - h1.astype(F32)).astype(BF)
        w1 = wr_ref[0]
        w2 = wr_ref[1]
        logits = (_dot(h2, w1) + _dot(h1, w2)) + _dot(h1, w1)
        lane = lax.broadcasted_iota(jnp.int32, logits.shape, 1)
        logits = jnp.where(lane < N_EXPERTS, logits, -jnp.inf)
        v1 = jnp.max(logits, axis=-1, keepdims=True)
        i1 = jnp.min(jnp.where(logits == v1, lane, LANES), axis=-1, keepdims=True)
        rest = jnp.where(lane == i1, -jnp.inf, logits)
        v2 = jnp.max(rest, axis=-1, keepdims=True)
        i2 = jnp.min(jnp.where(rest == v2, lane, LANES), axis=-1, keepdims=True)
        e2 = jnp.exp(v2 - v1)
        g1 = 1.0 / (1.0 + e2)
        g2 = e2 / (1.0 + e2)
        idx_ref[...] = jnp.where(lane == 0, i1, jnp.where(lane == 1, i2, 0))
        gate_ref[...] = jnp.where(lane == 0, g1, jnp.where(lane == 1, g2, 0.0))
    else:
        h_ref[...] = h.astype(h_ref.dtype)


def _merge(x2, z, og, ob, gn, wpa, wpb, wo, nf, wr=None):
    m = x2.shape[0]
    moe = wr is not None
    precise = wpa.dtype == F32
    tm = _tile(m, 320, 8)
    cb = D_MODEL
    row = lambda c: pl.BlockSpec((tm, cb), lambda i: (i, c))
    full = lambda a: pl.BlockSpec(a.shape, lambda i: (0,) * a.ndim)
    in_specs = [row(0), row(0), row(RA // cb), pl.BlockSpec((tm, FOX_W), lambda i: (i, 0)),
                row(GA // cb), row(GB // cb),
                full(gn), full(wpa), full(wpb), full(wo), full(nf)]
    args = [x2, og, z, ob, z, z, gn, wpa, wpb, wo, nf]
    out_specs = [row(0), row(0)]
    out_shape = [jax.ShapeDtypeStruct((m, D_MODEL), F32), jax.ShapeDtypeStruct((m, D_MODEL), F32 if (moe or precise) else BF)]
    if moe:
        in_specs.append(full(wr))
        args.append(wr)
        out_specs += [pl.BlockSpec((tm, LANES), lambda i: (i, 0))] * 2
        out_shape += [jax.ShapeDtypeStruct((m, LANES), jnp.int32), jax.ShapeDtypeStruct((m, LANES), F32)]
    return pl.pallas_call(
        functools.partial(_merge_kernel, moe=moe, precise=precise),
        grid=(m // tm,),
        in_specs=in_specs, out_specs=out_specs, out_shape=out_shape,
        compiler_params=_params("parallel"),
        name="merge",
    )(*args)


def _ffn_kernel(h_ref, x_ref, w1_ref, w3_ref, w2_ref, o_ref, acc, *, precise):
    j = pl.program_id(1)

    @pl.when(j == 0)
    def _():
        acc[...] = jnp.zeros_like(acc)

    h = h_ref[...]
    u = jax.nn.silu(_mm(h, w1_ref[...], precise)) * _mm(h, w3_ref[...], precise)
    acc[...] += _mm(u, w2_ref[...], precise)

    @pl.when(j == pl.num_programs(1) - 1)
    def _():
        o_ref[...] = x_ref[...] + acc[...]


def _ffn(h, x2, w1, w3, w2, precise):
    m = x2.shape[0]
    f = w1.shape[1]
    tm = _tile(m, 1280, 8)
    tf = _tile(f, 256, LANES)
    return pl.pallas_call(
        functools.partial(_ffn_kernel, precise=precise),
        grid=(m // tm, f // tf),
        in_specs=[
            pl.BlockSpec((tm, D_MODEL), lambda i, j: (i, 0)),
            pl.BlockSpec((tm, D_MODEL), lambda i, j: (i, 0)),
            pl.BlockSpec((D_MODEL, tf), lambda i, j: (0, j)),
            pl.BlockSpec((D_MODEL, tf), lambda i, j: (0, j)),
            pl.BlockSpec((tf, D_MODEL), lambda i, j: (j, 0)),
        ],
        out_specs=pl.BlockSpec((tm, D_MODEL), lambda i, j: (i, 0)),
        out_shape=jax.ShapeDtypeStruct((m, D_MODEL), F32),
        scratch_shapes=[pltpu.VMEM((tm, D_MODEL), F32)],
        compiler_params=_params("parallel", "arbitrary"),
        name="ffn",
    )(h, x2, w1, w3, w2)


def _moe_kernel(te_ref, tv_ref, h_ref, w1_ref, w3_ref, w2_ref, ys_ref, o_ref, hb, *, base):
    del te_ref, ys_ref
    j = pl.program_id(1)
    live = tv_ref[base + pl.program_id(0)] > 0

    @pl.when(j == 0)
    def _():
        o_ref[...] = jnp.zeros_like(o_ref)
        hb[...] = h_ref[...].astype(BF)

    @pl.when(live)
    def _():
        h = hb[...]
        u = jax.nn.silu(_mm(h, w1_ref[0], False)) * _mm(h, w3_ref[0], False)
        o_ref[...] += _mm(u, w2_ref[0], False)


def _moe_experts(tile_expert, tile_valid, hs, w1, w3, w2, tm, ys, chunk):
    tiles = hs.shape[0] // tm
    base = chunk * tiles
    f = w1.shape[2]
    tf = _tile(f, 512, LANES)
    nf = f // tf

    def fj(i, j, tv):
        return jnp.where(tv[base + i] > 0, j, nf - 1)

    grid_spec = pltpu.PrefetchScalarGridSpec(
        num_scalar_prefetch=2,
        grid=(tiles, nf),
        in_specs=[
            pl.BlockSpec((tm, D_MODEL), lambda i, j, te, tv: (i, 0)),
            pl.BlockSpec((1, D_MODEL, tf), lambda i, j, te, tv: (te[base + i], 0, fj(i, j, tv))),
            pl.BlockSpec((1, D_MODEL, tf), lambda i, j, te, tv: (te[base + i], 0, fj(i, j, tv))),
            pl.BlockSpec((1, tf, D_MODEL), lambda i, j, te, tv: (te[base + i], fj(i, j, tv), 0)),
            pl.BlockSpec(memory_space=pl.ANY),
        ],
        out_specs=pl.BlockSpec((tm, D_MODEL), lambda i, j, te, tv: (base + i, 0)),
        scratch_shapes=[pltpu.VMEM((tm, D_MODEL), BF)],
    )
    return pl.pallas_call(
        functools.partial(_moe_kernel, base=base),
        grid_spec=grid_spec,
        out_shape=jax.ShapeDtypeStruct(ys.shape, F32),
        input_output_aliases={6: 0},
        compiler_params=_params("parallel", "arbitrary"),
        name="moe_experts",
    )(tile_expert, tile_valid, hs, w1, w3, w2, ys)


def _moe_route(idx, tm):
    mt = idx.shape[0]
    e_flat = jnp.concatenate([idx[:, 0], idx[:, 1]])
    tok = jnp.concatenate([jnp.arange(mt, dtype=jnp.int32)] * 2)
    onehot = (e_flat[:, None] == jnp.arange(N_EXPERTS, dtype=jnp.int32)[None, :]).astype(jnp.int32)
    incl = jnp.cumsum(onehot, axis=0)
    rank = jnp.sum((incl - onehot) * onehot, axis=1)
    counts = incl[-1]
    tiles_per = (counts + tm - 1) // tm
    tile_end = jnp.cumsum(tiles_per)
    start = (tile_end - tiles_per) * tm
    dest = jnp.sum(onehot * start[None, :], axis=1) + rank
    n_tiles = -(-((2 * mt) // tm + N_EXPERTS) // MOE_CHUNKS) * MOE_CHUNKS
    slot_tok = jnp.zeros((n_tiles * tm,), jnp.int32).at[dest].set(
        tok, unique_indices=True, mode="promise_in_bounds")
    ti = jnp.arange(n_tiles, dtype=jnp.int32)
    tile_valid = (ti < tile_end[-1]).astype(jnp.int32)
    te = jnp.sum((ti[:, None] >= tile_end[None, :]).astype(jnp.int32), axis=1)
    last_e = jnp.sum((tile_end[-1] - 1 >= tile_end).astype(jnp.int32))
    tile_expert = jnp.where(tile_valid > 0, te, last_e).astype(jnp.int32)
    return slot_tok, tile_expert, tile_valid, dest[:mt], dest[mt:]


def _combine_kernel(*refs, final):
    if final:
        x_ref, a_ref, b_ref, g_ref, gf_ref, o_ref = refs
    else:
        x_ref, a_ref, b_ref, g_ref, o_ref = refs
    g = g_ref[...]
    y = x_ref[...] + (g[:, 0:1] * a_ref[...] + g[:, 1:2] * b_ref[...])
    o_ref[...] = _rms(y, gf_ref[...]) if final else y


def _combine(x2, a, b, gate, gfin=None):
    m = x2.shape[0]
    tm = _tile(m, 640, 8)
    spec = pl.BlockSpec((tm, D_MODEL), lambda i: (i, 0))
    in_specs = [spec] * 3 + [pl.BlockSpec((tm, LANES), lambda i: (i, 0))]
    args = [x2, a, b, gate]
    if gfin is not None:
        in_specs.append(pl.BlockSpec((1, D_MODEL), lambda i: (0, 0)))
        args.append(gfin)
    return pl.pallas_call(
        functools.partial(_combine_kernel, final=gfin is not None),
        grid=(m // tm,), in_specs=in_specs, out_specs=spec,
        out_shape=jax.ShapeDtypeStruct((m, D_MODEL), F32),
        compiler_params=_params("parallel"), name="combine",
    )(*args)


def _final_kernel(x_ref, g_ref, o_ref):
    o_ref[...] = _rms(x_ref[...], g_ref[...])


def _final_norm(x2, g):
    m = x2.shape[0]
    tm = _tile(m, 640, 8)
    spec = pl.BlockSpec((tm, D_MODEL), lambda i: (i, 0))
    return pl.pallas_call(
        _final_kernel, grid=(m // tm,),
        in_specs=[spec, pl.BlockSpec((1, D_MODEL), lambda i: (0, 0))], out_specs=spec,
        out_shape=jax.ShapeDtypeStruct((m, D_MODEL), F32),
        compiler_params=_params("parallel"), name="final_norm",
    )(x2, g)


def _pack_w_in(w):
    offs = [0]
    for s in IN_SPLITS:
        offs.append(offs[-1] + s)
    seg = [w[:, offs[i]:offs[i + 1]] for i in range(len(IN_SPLITS))]
    qa, ka, va, ra, alr, qb, kb, vb, fb, ga, gb = seg
    wm = jnp.concatenate([va, ra, ga, gb, qa, ka, qb, kb, vb], axis=1)
    ws = jnp.concatenate([fb, alr, jnp.zeros((D_MODEL, LANES - FOX_HEADS - GLA_RANK), w.dtype)], axis=1)
    wkv_t = jnp.concatenate([kb, vb], axis=1).T.astype(BF)
    return wm, ws, wkv_t


def kernel(x_prompt, x_sample, cache_k, cache_v, cache_logf, state_gla, page_table, meta_tokens, norm_mix, w_in,
           w_alpha2, b_alpha, b_f, gla_norm, w_proj_a, w_proj_b, w_out, norm_ffn, w_ffn1, w_ffn3, w_ffn2, w_router,
           w_exp1, w_exp3, w_exp2, norm_final):
    b, seq, _ = x_prompt.shape
    db, n_dec, _ = x_sample.shape
    depth = w_in.shape[0]
    n_pool = cache_k.shape[1]
    n_valid = N_META + seq
    t = -(-n_valid // BLK) * BLK
    mp, msz = b * t, db * n_dec

    xp = jnp.concatenate([jnp.broadcast_to(meta_tokens[None].astype(F32), (b, N_META, D_MODEL)), x_prompt,
                          jnp.zeros((b, t - n_valid, D_MODEL), F32)], axis=1).reshape(mp, D_MODEL)
    xs = x_sample.reshape(msz, D_MODEL)
    cache_kt = jnp.transpose(cache_k, (0, 1, 3, 4, 2)).reshape(depth, n_pool, FOX_W, BLK)
    cache_vt = jnp.transpose(cache_v, (0, 1, 3, 4, 2)).reshape(depth, n_pool, FOX_W, BLK)
    page_rows = depth * n_pool * FOX_HEADS
    rows_per = _tile(page_rows, 2048, 8)
    lf_pages = jnp.transpose(cache_logf, (0, 1, 3, 2)).reshape(page_rows // rows_per, rows_per, BLK)
    cpage = _cumsum_lanes(lf_pages, running=False).reshape(depth, n_pool, FOX_HEADS, BLK)
    page_flat = page_table.reshape(-1).astype(jnp.int32)

    outs = {k: [] for k in ("kp", "vp", "lp", "sp", "ks", "vs", "ls", "ss")}
    y_prompt = y_sample = None
    for l in range(depth):
        wm32, ws32, wkv_t = _pack_w_in(w_in[l])
        wm, ws = wm32[:, :NZ_PROMPT].astype(BF), ws32.astype(BF)
        wst = ws.T
        wa32 = jnp.zeros((LANES, GLA_QK), F32).at[FOX_HEADS:FOX_HEADS + GLA_RANK].set(w_alpha2[l])
        wa = wa32.astype(BF)
        ba = b_alpha[l].reshape(1, GLA_QK)
        bfp = jnp.zeros((1, LANES), F32).at[0, :FOX_HEADS].set(b_f[l])
        bfc = b_f[l].reshape(FOX_HEADS, 1)
        g_mix = norm_mix[l].reshape(1, D_MODEL)
        gn = gla_norm[l].reshape(1, GLA_DV)
        wpa, wpb, wo = w_proj_a[l].astype(BF), w_proj_b[l].astype(BF), w_out[l].astype(BF)
        nf = norm_ffn[l].reshape(1, D_MODEL)
        is_moe = l % 2 == 1
        last = l == depth - 1
        jx = l // 2

        z_p, la_p, kt_p, vt_p, lft_p = _inproj(xp, g_mix, wm, ws, wa, ba, bfp,
                                               head_major=(b, wkv_t, wst, bfc))
        z3 = z_p.reshape(b, t, NZ_PROMPT)
        c_rows = _cumsum_lanes(lft_p, running=True)
        ob_p = _fox_prompt(z3, kt_p, vt_p, c_rows.reshape(b, FOX_HEADS // 2, 2, t))
        og_p, s_p = _gla_prompt(z3, la_p.reshape(b, t, GLA_QK), n_valid)
        z_s, la_s, lf_s = _inproj(xs, g_mix, wm32, ws32, wa32, ba, bfp)
        zs3 = z_s.reshape(db, n_dec, NZ)
        og_s, s_s = _gla_sample(zs3, la_s.reshape(db, n_dec, GLA_QK), state_gla[l])
        kn3 = zs3[:, :, KB:KB + FOX_W]
        vn3 = zs3[:, :, VB:VB + FOX_W]
        ob_s = _fox_sample(page_flat, cache_kt, cache_vt, cpage, l, zs3, kn3, vn3,
                           lf_s.reshape(db, n_dec, LANES))

        outs["kp"].append(kt_p[..., :n_valid])
        outs["vp"].append(vt_p[..., :n_valid])
        outs["lp"].append(lft_p[..., :n_valid])
        outs["sp"].append(s_p)
        outs["ks"].append(kn3.reshape(db, n_dec, FOX_HEADS, FOX_HD))
        outs["vs"].append(vn3.reshape(db, n_dec, FOX_HEADS, FOX_HD))
        outs["ls"].append(lf_s.reshape(db, n_dec, LANES)[:, :, :FOX_HEADS])
        outs["ss"].append(s_s)

        og_p2, ob_p2 = og_p.reshape(mp, GLA_V), ob_p.reshape(mp, FOX_W)
        og_s2, ob_s2 = og_s.reshape(msz, GLA_V), ob_s.reshape(msz, FOX_W)
        if not is_moe:
            xp, h_p = _merge(xp, z_p, og_p2, ob_p2, gn, wpa, wpb, wo, nf)
            xs, h_s = _merge(xs, z_s, og_s2, ob_s2, gn, w_proj_a[l], w_proj_b[l], w_out[l], nf)
            xs = _ffn(h_s, xs, w_ffn1[jx], w_ffn3[jx], w_ffn2[jx], True)
            xp = _ffn(h_p, xp, w_ffn1[jx], w_ffn3[jx], w_ffn2[jx], False)
        else:
            wr = jnp.zeros((D_MODEL, LANES), F32).at[:, :N_EXPERTS].set(w_router[jx])
            wr1 = wr.astype(BF)
            wr = jnp.stack([wr1, (wr - wr1.astype(F32)).astype(BF)])
            xp, h_p, idx_p, gate_p = _merge(xp, z_p, og_p2, ob_p2, gn, wpa, wpb, wo, nf, wr)
            xs, h_s, idx_s, gate_s = _merge(xs, z_s, og_s2, ob_s2, gn, w_proj_a[l], w_proj_b[l], w_out[l], nf, wr)
            h_all = jnp.concatenate([h_p, h_s], axis=0)
            idx = jnp.concatenate([idx_p[:, :2], idx_s[:, :2]], axis=0)
            tm = 1024 if 2 * (mp + msz) >= 8 * 1024 else 256
            slot_tok, tile_e, tile_v, d1, d2 = _moe_route(idx, tm)
            rows = lambda a, i: a.at[i].get(mode="promise_in_bounds")
            ys = jnp.zeros((slot_tok.shape[0], D_MODEL), F32)
            per = slot_tok.shape[0] // MOE_CHUNKS
            for c in range(MOE_CHUNKS):
                hs = rows(h_all, slot_tok[c * per:(c + 1) * per])
                ys = _moe_experts(tile_e, tile_v, hs, w_exp1[jx], w_exp3[jx], w_exp2[jx], tm, ys, c)
            gfin = norm_final.reshape(1, D_MODEL) if last else None
            yp = _combine(xp, rows(ys, d1[:mp]), rows(ys, d2[:mp]), gate_p, gfin)
            ysm = _combine(xs, rows(ys, d1[mp:]), rows(ys, d2[mp:]), gate_s, gfin)
            if last:
                y_prompt, y_sample = yp, ysm
            else:
                xp, xs = yp, ysm

    if y_prompt is None:
        g_fin = norm_final.reshape(1, D_MODEL)
        y_prompt, y_sample = _final_norm(xp, g_fin), _final_norm(xs, g_fin)
    y_prompt = y_prompt.reshape(b, t, D_MODEL)[:, N_META:n_valid]
    y_sample = y_sample.reshape(db, n_dec, D_MODEL)
    st = jnp.stack
    return (y_prompt, y_sample,
            jnp.transpose(st(outs["kp"]), (0, 1, 4, 2, 3)), jnp.transpose(st(outs["vp"]), (0, 1, 4, 2, 3)),
            jnp.transpose(st(outs["lp"]), (0, 1, 3, 2)), st(outs["sp"]),
            st(outs["ks"]), st(outs["vs"]), st(outs["ls"]), st(outs["ss"]))
```
